```python
import math
import jax, jax.numpy as jnp
from jax import lax
import numpy as np

D_MODEL = 1024
BATCH = 2
SEQ = 8192
DEPTH = 1

N_MEM = 256
GDN_HEADS = 4
GDN_DK = 128
GDN_DV = 128
GDN_CONV = 4
GDN_CHUNK = 64
DIFF_HEADS = 4
DIFF_DQK = 64
DIFF_DV = 2 * DIFF_DQK
Q_BLOCK = 128
XA_HEADS = 4
XA_DH = D_MODEL // XA_HEADS
N_EXPERTS = 32
TOP_K = 4
D_FF = D_MODEL
SWIGLU_ALPHA = 1.702
SWIGLU_LIMIT = 7.0
MOE_BLOCK = 128
LN_EPS = 1e-5
RMS_EPS = 1e-6
DEEPNORM_ALPHA = (2 * DEPTH) ** 0.25
DEEPNORM_BETA = (8 * DEPTH) ** -0.25
GDN_QK_W = GDN_HEADS * GDN_DK
GDN_V_W = GDN_HEADS * GDN_DV
DIFF_QK_W = DIFF_HEADS * 2 * DIFF_DQK
DIFF_V_W = DIFF_HEADS * DIFF_DV
MIX_WIDTH = GDN_V_W + DIFF_V_W
IN_SIZES = (GDN_QK_W, GDN_QK_W, GDN_V_W, GDN_V_W, GDN_HEADS, GDN_HEADS, DIFF_QK_W, DIFF_QK_W, DIFF_V_W)
IN_COLS = sum(IN_SIZES)

kernel_name = 'hybrid_gdn_diffattn_memxattn_moe_deepnorm'


def split_cols(t, sizes):
    out, off = [], 0
    for s in sizes:
        out.append(t[..., off:off + s])
        off += s
    return out


def layer_norm(x, g, b):
    xf = x.astype(jnp.float32)
    mu = jnp.mean(xf, -1, keepdims=True)
    var = jnp.mean(jnp.square(xf - mu), -1, keepdims=True)
    return ((xf - mu) * lax.rsqrt(var + LN_EPS) * g.astype(jnp.float32) + b.astype(jnp.float32)).astype(x.dtype)


def rms_norm(x, w):
    xf = x.astype(jnp.float32)
    return xf * lax.rsqrt(jnp.mean(jnp.square(xf), -1, keepdims=True) + RMS_EPS) * w.astype(jnp.float32)


def l2_normalize(x):
    xf = x.astype(jnp.float32)
    return xf * lax.rsqrt(jnp.sum(jnp.square(xf), -1, keepdims=True) + RMS_EPS)


def causal_depthwise_conv(x, w):
    return lax.conv_general_dilated(x, w.astype(x.dtype), window_strides=(1,),
                                    padding=((GDN_CONV - 1, 0),),
                                    dimension_numbers=('NWC', 'WIO', 'NWC'),
                                    feature_group_count=x.shape[-1])


def gated_delta_rule_chunked(q, k, v, g, beta):
    B_, S_, H, DK = q.shape
    DV = v.shape[-1]
    C = GDN_CHUNK
    N = S_ // C
    f32 = jnp.float32

    def to_chunks(t):
        t = t.astype(f32).reshape((B_, N, C, H) + t.shape[3:])
        return jnp.moveaxis(t, 3, 1)

    q = to_chunks(q) * (DK ** -0.5)
    k = to_chunks(k)
    v = to_chunks(v)
    beta = to_chunks(beta)
    g = jnp.cumsum(to_chunks(g), axis=-1)
    causal = jnp.tril(jnp.ones((C, C), bool))
    strict = jnp.tril(jnp.ones((C, C), bool), -1)
    gdiff = g[..., :, None] - g[..., None, :]
    decay = jnp.where(causal, jnp.exp(jnp.where(causal, gdiff, 0.0)), 0.0)
    k_beta = k * beta[..., None]
    v_beta = v * beta[..., None]
    lmat = jnp.where(strict, jnp.einsum('bhncd,bhnsd->bhncs', k_beta, k) * decay, 0.0)
    amat = lmat + jnp.eye(C, dtype=f32)
    rhs = jnp.concatenate([v_beta, k_beta * jnp.exp(g)[..., None]], -1)
    sol = lax.linalg.triangular_solve(amat, rhs, left_side=True, lower=True, unit_diagonal=True)
    u = sol[..., :DV]
    w = sol[..., DV:]
    attn_intra = jnp.where(causal, jnp.einsum('bhncd,bhnsd->bhncs', q, k) * decay, 0.0)
    q_decay = q * jnp.exp(g)[..., None]
    k_tail = k * jnp.exp(g[..., -1:] - g)[..., None]
    g_last = jnp.exp(g[..., -1])

    def step(state, inp):
        u_c, w_c, qd_c, ai_c, kt_c, gl_c = inp
        v_new = u_c - jnp.einsum('bhck,bhkv->bhcv', w_c, state)
        out = jnp.einsum('bhck,bhkv->bhcv', qd_c, state) + jnp.einsum('bhcs,bhsv->bhcv', ai_c, v_new)
        state = state * gl_c[..., None, None] + jnp.einsum('bhck,bhcv->bhkv', kt_c, v_new)
        return state, out

    xs = tuple(jnp.moveaxis(t, 2, 0) for t in (u, w, q_decay, attn_intra, k_tail, g_last))
    state0 = jnp.zeros((B_, H, DK, DV), f32)
    _, out = lax.scan(step, state0, xs)
    return jnp.transpose(out, (1, 0, 3, 2, 4)).reshape(B_, S_, H, DV)


def differential_attention(q, k, v, lam):
    B_, S_, H = q.shape[:3]
    nb = S_ // Q_BLOCK
    scale = DIFF_DQK ** -0.5
    qb = jnp.moveaxis(q.reshape((B_, nb, Q_BLOCK) + q.shape[2:]), 1, 0)
    kpos = jnp.arange(S_)

    def block(args):
        q_blk, i = args
        qpos = i * Q_BLOCK + jnp.arange(Q_BLOCK)
        s = jnp.einsum('bqhcd,bkhcd->bhcqk', q_blk, k).astype(jnp.float32) * scale
        s = jnp.where(kpos[None, :] <= qpos[:, None], s, -jnp.inf)
        p = jax.nn.softmax(s, axis=-1)
        a = p[:, :, 0] - lam * p[:, :, 1]
        return jnp.einsum('bhqk,bkhv->bqhv', a.astype(v.dtype), v)

    out = lax.map(block, (qb, jnp.arange(nb)))
    return jnp.moveaxis(out, 0, 1).reshape(B_, S_, H, v.shape[-1])


def hybrid_mixer(x, w_in, conv_w, a_log, dt_bias, gdn_norm_w, lq1, lk1, lq2, lk2, diff_norm_w, w_out, lambda_init):
    B_, S_, _ = x.shape
    f32 = jnp.float32
    proj = x @ w_in
    gq, gk, gv, gz, gb, ga, dq, dk, dv = split_cols(proj, IN_SIZES)
    qkv = jax.nn.silu(causal_depthwise_conv(jnp.concatenate([gq, gk, gv], -1), conv_w))
    gq, gk, gv = split_cols(qkv, (GDN_QK_W, GDN_QK_W, GDN_V_W))
    q = l2_normalize(gq.reshape(B_, S_, GDN_HEADS, GDN_DK))
    k = l2_normalize(gk.reshape(B_, S_, GDN_HEADS, GDN_DK))
    v = gv.reshape(B_, S_, GDN_HEADS, GDN_DV).astype(f32)
    beta = jax.nn.sigmoid(gb.astype(f32))
    g = -jnp.exp(a_log.astype(f32)) * jax.nn.softplus(ga.astype(f32) + dt_bias.astype(f32))
    o = gated_delta_rule_chunked(q, k, v, g, beta)
    o = rms_norm(o, gdn_norm_w) * jax.nn.silu(gz.reshape(B_, S_, GDN_HEADS, GDN_DV).astype(f32))
    o_gdn = o.reshape(B_, S_, GDN_V_W).astype(x.dtype)
    lam = (jnp.exp(jnp.sum(lq1.astype(f32) * lk1.astype(f32)))
           - jnp.exp(jnp.sum(lq2.astype(f32) * lk2.astype(f32))) + lambda_init)
    od = differential_attention(dq.reshape(B_, S_, DIFF_HEADS, 2, DIFF_DQK),
                                dk.reshape(B_, S_, DIFF_HEADS, 2, DIFF_DQK),
                                dv.reshape(B_, S_, DIFF_HEADS, DIFF_DV), lam)
    od = rms_norm(od, diff_norm_w) * (1.0 - lambda_init)
    o_diff = od.reshape(B_, S_, DIFF_V_W).astype(x.dtype)
    return jnp.concatenate([o_gdn, o_diff], -1) @ w_out


def mem_cross_attention(x, mem, wq, wk, wv, wo):
    B_, S_, D = x.shape
    M = mem.shape[1]
    q = (x @ wq).reshape(B_, S_, XA_HEADS, XA_DH)
    k = (mem @ wk).reshape(B_, M, XA_HEADS, XA_DH)
    v = (mem @ wv).reshape(B_, M, XA_HEADS, XA_DH)
    s = jnp.einsum('bshd,bmhd->bhsm', q, k).astype(jnp.float32) * (XA_DH ** -0.5)
    p = jax.nn.softmax(s, axis=-1)
    o = jnp.einsum('bhsm,bmhd->bshd', p.astype(v.dtype), v).reshape(B_, S_, D)
    return o @ wo


def clamped_swiglu(h_gate, h_lin):
    h_gate = jnp.minimum(h_gate, SWIGLU_LIMIT)
    h_lin = jnp.clip(h_lin, -SWIGLU_LIMIT, SWIGLU_LIMIT)
    return h_gate * jax.nn.sigmoid(SWIGLU_ALPHA * h_gate) * (h_lin + 1.0)


def moe_ffn(x, router_w, router_b, w_gate, b_gate, w_up, b_up, w_down, b_down):
    B_, S_, D = x.shape
    xt = x.reshape(-1, D)
    T = xt.shape[0]
    logits = (xt @ router_w + router_b).astype(jnp.float32)
    top_val, top_idx = lax.top_k(logits, TOP_K)
    gate = jax.nn.softmax(top_val, axis=-1)
    A = T * TOP_K
    flat_e = top_idx.reshape(-1)
    order = jnp.argsort(flat_e)
    sorted_e = flat_e[order]
    token_of = order // TOP_K
    counts = jnp.bincount(flat_e, length=N_EXPERTS)
    padded = (counts + MOE_BLOCK - 1) // MOE_BLOCK * MOE_BLOCK
    start = jnp.cumsum(counts) - counts
    pad_end = jnp.cumsum(padded)
    pad_start = pad_end - padded
    dest = pad_start[sorted_e] + (jnp.arange(A) - start[sorted_e])
    n_blocks = -(-A // MOE_BLOCK) + N_EXPERTS
    P = n_blocks * MOE_BLOCK
    xs = jnp.zeros((P, D), xt.dtype).at[dest].set(xt[token_of])
    block_e = jnp.minimum(jnp.searchsorted(pad_end, jnp.arange(n_blocks) * MOE_BLOCK, side='right'), N_EXPERTS - 1)

    def expert_block(args):
        xb, e = args
        h = clamped_swiglu(xb @ w_gate[e] + b_gate[e], xb @ w_up[e] + b_up[e])
        return h @ w_down[e] + b_down[e]

    ys = lax.map(expert_block, (xs.reshape(n_blocks, MOE_BLOCK, D), block_e)).reshape(P, D)
    y_assign = ys[dest] * gate.reshape(-1)[order][:, None].astype(ys.dtype)
    y = jnp.zeros_like(xt).at[token_of].add(y_assign)
    return y.reshape(B_, S_, D)


def setup_inputs(seed: int = 0) -> dict:
    key = jax.random.key(seed)
    ks = jax.random.split(key, 40)
    f32 = jnp.float32
    L = DEPTH

    def nrm(k, shape, scale):
        return jax.random.normal(k, shape, f32) * scale

    dt = jnp.exp(jax.random.uniform(ks[4], (L, GDN_HEADS), f32, math.log(1e-3), math.log(1e-1)))
    return {
        'x': nrm(ks[0], (BATCH, SEQ, D_MODEL), 1.0),
        'mem': nrm(ks[1], (BATCH, N_MEM, D_MODEL), 1.0),
        'w_in': nrm(ks[2], (L, D_MODEL, IN_COLS), D_MODEL ** -0.5),
        'gdn_conv_w': nrm(ks[3], (L, GDN_CONV, 1, 2 * GDN_QK_W + GDN_V_W), GDN_CONV ** -0.5),
        'gdn_a_log': jnp.log(jax.random.uniform(ks[5], (L, GDN_HEADS), f32, 1.0, 16.0)),
        'gdn_dt_bias': dt + jnp.log(-jnp.expm1(-dt)),
        'gdn_norm_w': 1.0 + nrm(ks[6], (L, GDN_DV), 0.02),
        'diff_lq1': nrm(ks[7], (L, DIFF_DQK), 0.1),
        'diff_lk1': nrm(ks[8], (L, DIFF_DQK), 0.1),
        'diff_lq2': nrm(ks[9], (L, DIFF_DQK), 0.1),
        'diff_lk2': nrm(ks[10], (L, DIFF_DQK), 0.1),
        'diff_norm_w': 1.0 + nrm(ks[11], (L, DIFF_DV), 0.02),
        'w_out': nrm(ks[12], (L, MIX_WIDTH, D_MODEL), MIX_WIDTH ** -0.5 * DEEPNORM_BETA),
        'ln1_g': 1.0 + nrm(ks[13], (L, D_MODEL), 0.02),
        'ln1_b': nrm(ks[14], (L, D_MODEL), 0.02),
        'xa_wq': nrm(ks[15], (L, D_MODEL, D_MODEL), D_MODEL ** -0.5),
        'xa_wk': nrm(ks[16], (L, D_MODEL, D_MODEL), D_MODEL ** -0.5),
        'xa_wv': nrm(ks[17], (L, D_MODEL, D_MODEL), D_MODEL ** -0.5),
        'xa_wo': nrm(ks[18], (L, D_MODEL, D_MODEL), D_MODEL ** -0.5 * DEEPNORM_BETA),
        'ln2_g': 1.0 + nrm(ks[19], (L, D_MODEL), 0.02),
        'ln2_b': nrm(ks[20], (L, D_MODEL), 0.02),
        'router_w': nrm(ks[21], (L, D_MODEL, N_EXPERTS), D_MODEL ** -0.5),
        'router_b': nrm(ks[22], (L, N_EXPERTS), 0.01),
        'exp_w_gate': nrm(ks[23], (L, N_EXPERTS, D_MODEL, D_FF), D_MODEL ** -0.5),
        'exp_b_gate': nrm(ks[24], (L, N_EXPERTS, D_FF), 0.01),
        'exp_w_up': nrm(ks[25], (L, N_EXPERTS, D_MODEL, D_FF), D_MODEL ** -0.5),
        'exp_b_up': nrm(ks[26], (L, N_EXPERTS, D_FF), 0.01),
        'exp_w_down': nrm(ks[27], (L, N_EXPERTS, D_FF, D_MODEL), D_FF ** -0.5 * DEEPNORM_BETA),
        'exp_b_down': nrm(ks[28], (L, N_EXPERTS, D_MODEL), 0.01),
        'ln3_g': 1.0 + nrm(ks[29], (L, D_MODEL), 0.02),
        'ln3_b': nrm(ks[30], (L, D_MODEL), 0.02),
    }


def reference(x, mem, w_in, gdn_conv_w, gdn_a_log, gdn_dt_bias, gdn_norm_w, diff_lq1, diff_lk1, diff_lq2, diff_lk2,
              diff_norm_w, w_out, ln1_g, ln1_b, xa_wq, xa_wk, xa_wv, xa_wo, ln2_g, ln2_b, router_w, router_b,
              exp_w_gate, exp_b_gate, exp_w_up, exp_b_up, exp_w_down, exp_b_down, ln3_g, ln3_b):
    for l in range(DEPTH):
        lambda_init = 0.8 - 0.6 * math.exp(-0.3 * l)
        h = hybrid_mixer(x, w_in[l], gdn_conv_w[l], gdn_a_log[l], gdn_dt_bias[l], gdn_norm_w[l],
                         diff_lq1[l], diff_lk1[l], diff_lq2[l], diff_lk2[l], diff_norm_w[l], w_out[l], lambda_init)
        x = layer_norm(DEEPNORM_ALPHA * x + h, ln1_g[l], ln1_b[l])
        h = mem_cross_attention(x, mem, xa_wq[l], xa_wk[l], xa_wv[l], xa_wo[l])
        x = layer_norm(DEEPNORM_ALPHA * x + h, ln2_g[l], ln2_b[l])
        h = moe_ffn(x, router_w[l], router_b[l], exp_w_gate[l], exp_b_gate[l], exp_w_up[l], exp_b_up[l],
                    exp_w_down[l], exp_b_down[l])
        x = layer_norm(DEEPNORM_ALPHA * x + h, ln3_g[l], ln3_b[l])
    return x
```

```python
import functools
import math

import jax
import jax.numpy as jnp
from jax import lax
from jax.experimental import pallas as pl
from jax.experimental.pallas import tpu as pltpu

F32 = jnp.float32
BF16 = jnp.bfloat16
HIGHEST = lax.Precision.HIGHEST

GDN_HEADS = 4
GDN_DK = 128
GDN_DV = 128
GDN_CONV = 4
GDN_CHUNK = 64
DIFF_HEADS = 4
DIFF_DQK = 64
DIFF_DV = 128
XA_HEADS = 4
N_EXPERTS = 32
TOP_K = 4
SWIGLU_ALPHA = 1.702
SWIGLU_LIMIT = 7.0
LN_EPS = 1e-5
RMS_EPS = 1e-6
DEPTH = 1
DEEPNORM_ALPHA = (2 * DEPTH) ** 0.25

GDN_W = GDN_HEADS * GDN_DK
DIFF_W = DIFF_HEADS * 2 * DIFF_DQK
SMALL_W = 128

LANES = 128
SUBLANES = 8
VMEM_LIMIT = 56 * 1024 * 1024

NEG_BIG = -1e30


def _dot(a, b):
    return jnp.dot(a, b, preferred_element_type=F32)


def _dot_nt(a, b):
    return lax.dot_general(a, b, (((1,), (1,)), ((), ())), preferred_element_type=F32)


def _dot_tn(a, b):
    return lax.dot_general(a, b, (((0,), (0,)), ((), ())), preferred_element_type=F32)


def _layer_norm(y, g, b):
    mu = jnp.mean(y, axis=-1, keepdims=True)
    d = y - mu
    var = jnp.mean(d * d, axis=-1, keepdims=True)
    return d * lax.rsqrt(var + LN_EPS) * g + b


def _sigmoid(x):
    return 1.0 / (1.0 + jnp.exp(-x))


def _inproj_kernel(x_ref, wg_ref, wd_ref, ws_ref, g_ref, d_ref, s_ref):
    xb = x_ref[...].astype(BF16)
    g_ref[...] = _dot(xb, wg_ref[...]).astype(BF16)
    d_ref[...] = _dot(xb, wd_ref[...]).astype(BF16)
    s_ref[...] = _dot(xb, ws_ref[...])


def _inproj(x2d, wg, wd, ws, tm):
    t, d = x2d.shape
    full = lambda a: pl.BlockSpec(a.shape, lambda i: (0, 0))
    return pl.pallas_call(
        _inproj_kernel,
        grid=(t // tm,),
        in_specs=[pl.BlockSpec((tm, d), lambda i: (i, 0)), full(wg), full(wd), full(ws)],
        out_specs=[pl.BlockSpec((tm, wg.shape[1]), lambda i: (i, 0)),
                   pl.BlockSpec((tm, wd.shape[1]), lambda i: (i, 0)),
                   pl.BlockSpec((tm, ws.shape[1]), lambda i: (i, 0))],
        out_shape=[jax.ShapeDtypeStruct((t, wg.shape[1]), BF16),
                   jax.ShapeDtypeStruct((t, wd.shape[1]), BF16),
                   jax.ShapeDtypeStruct((t, ws.shape[1]), F32)],
        compiler_params=pltpu.CompilerParams(dimension_semantics=("parallel",),
                                             vmem_limit_bytes=VMEM_LIMIT),
        name="inproj",
    )(x2d, wg, wd, ws)


def _gdn_kernel(q_ref, k_ref, v_ref, z_ref, hq_ref, hk_ref, hv_ref, s_ref, cw_ref, ab_ref, nw_ref,
                o_ref, state_ref, *, ts):
    c_len = GDN_CHUNK
    n_chunks = ts // c_len
    i = pl.program_id(1)

    @pl.when(i == 0)
    def _():
        state_ref[...] = jnp.zeros_like(state_ref)

    row8 = lax.broadcasted_iota(jnp.int32, (SUBLANES, GDN_W), 0)
    first = i == 0

    def conv_silu(x_ref_, h_ref_, col0):
        x = x_ref_[0].astype(F32)
        hb = jnp.where(first, 0.0, h_ref_[0].astype(F32))
        w = cw_ref[:, col0:col0 + GDN_W]
        y = x * w[GDN_CONV - 1:GDN_CONV, :]
        for j in range(1, GDN_CONV):
            xr = pltpu.roll(x, j, 0)
            fix = pltpu.roll(hb, j, 0)
            top = jnp.where(row8 < j, fix, xr[0:SUBLANES])
            xr = jnp.concatenate([top, xr[SUBLANES:]], axis=0)
            y = y + xr * w[GDN_CONV - 1 - j:GDN_CONV - j, :]
        return y * _sigmoid(y)

    qa = conv_silu(q_ref, hq_ref, 0)
    ka = conv_silu(k_ref, hk_ref, GDN_W)
    va = conv_silu(v_ref, hv_ref, 2 * GDN_W)

    sg = s_ref[0]
    beta_all = _sigmoid(sg)
    sp_in = sg + ab_ref[1:2, :]
    softplus = jnp.maximum(sp_in, 0.0) + jnp.log(1.0 + jnp.exp(-jnp.abs(sp_in)))
    g_step = ab_ref[0:1, :] * softplus

    r = lax.broadcasted_iota(jnp.int32, (ts, ts), 0)
    c = lax.broadcasted_iota(jnp.int32, (ts, ts), 1)
    tri = jnp.where((r // c_len) == (c // c_len), jnp.where(c <= r, 1.0, 0.0), 0.0)
    gc = jnp.dot(tri, g_step, precision=HIGHEST, preferred_element_type=F32)

    ri = lax.broadcasted_iota(jnp.int32, (c_len, c_len), 0)
    ci = lax.broadcasted_iota(jnp.int32, (c_len, c_len), 1)
    causal = ci <= ri
    strict = ci < ri

    gct = [gc[cc * c_len:(cc + 1) * c_len, :].T for cc in range(n_chunks)]
    nw = nw_ref[...]

    for h in range(GDN_HEADS):
        lo, hi = h * GDN_DK, (h + 1) * GDN_DK
        qh = qa[:, lo:hi]
        kh = ka[:, lo:hi]
        vh = va[:, lo:hi]
        qh = qh * (lax.rsqrt(jnp.sum(qh * qh, axis=-1, keepdims=True) + RMS_EPS) * (GDN_DK ** -0.5))
        kh = kh * lax.rsqrt(jnp.sum(kh * kh, axis=-1, keepdims=True) + RMS_EPS)
        beta = beta_all[:, h:h + 1]
        gcol_all = gc[:, GDN_HEADS + h:GDN_HEADS + h + 1]

        pre = []
        for cc in range(n_chunks):
            r0, r1 = cc * c_len, (cc + 1) * c_len
            qc, kc, vc = qh[r0:r1], kh[r0:r1], vh[r0:r1]
            bcol = beta[r0:r1]
            gcol = gcol_all[r0:r1]
            grow = gct[cc][GDN_HEADS + h:GDN_HEADS + h + 1, :]
            decay = jnp.where(causal, jnp.exp(jnp.where(causal, gcol - grow, 0.0)), 0.0)
            kb = kc * bcol
            vb = vc * bcol
            eg = jnp.exp(gcol)
            kbe = kb * eg
            m = -jnp.where(strict, _dot_nt(kb, kc) * decay, 0.0)
            y = m
            for _ in range(5):
                m = _dot(m, m)
                y = y + m + _dot(y, m)
            u = vb + _dot(y, vb)
            w = kbe + _dot(y, kbe)
            ai = jnp.where(causal, _dot_nt(qc, kc) * decay, 0.0)
            qd = qc * eg
            g_last = gcol[c_len - 1:c_len, :]
            kt = kc * jnp.exp(g_last - gcol)
            pre.append((u, w, ai, qd, kt, jnp.exp(g_last)))

        state = state_ref[h]
        outs = []
        for cc in range(n_chunks):
            u, w, ai, qd, kt, gl = pre[cc]
            v_new = u - _dot(w, state)
            outs.append(_dot(qd, state) + _dot(ai, v_new))
            state = state * gl + _dot_tn(kt, v_new)
        state_ref[h] = state
        o = jnp.concatenate(outs, axis=0)
        o = o * lax.rsqrt(jnp.mean(o * o, axis=-1, keepdims=True) + RMS_EPS) * nw
        z = z_ref[0, :, lo:hi].astype(F32)
        o_ref[0, :, lo:hi] = (o * (z * _sigmoid(z))).astype(o_ref.dtype)


def _gdn(g3, s3, conv_w, ab, nw, ts):
    b, s, _ = g3.shape
    nb = GDN_W // GDN_W
    del nb
    halo_rows = SUBLANES
    per_block = ts // halo_rows

    def blk(col):
        return pl.BlockSpec((1, ts, GDN_W), lambda bb, i: (bb, i, col))

    def halo(col):
        return pl.BlockSpec((1, halo_rows, GDN_W),
                            lambda bb, i: (bb, jnp.maximum(i * per_block - 1, 0), col))

    return pl.pallas_call(
        functools.partial(_gdn_kernel, ts=ts),
        grid=(b, s // ts),
        in_specs=[blk(0), blk(1), blk(2), blk(3), halo(0), halo(1), halo(2),
                  pl.BlockSpec((1, ts, SMALL_W), lambda bb, i: (bb, i, 0)),
                  pl.BlockSpec(conv_w.shape, lambda bb, i: (0, 0)),
                  pl.BlockSpec(ab.shape, lambda bb, i: (0, 0)),
                  pl.BlockSpec(nw.shape, lambda bb, i: (0, 0))],
        out_specs=pl.BlockSpec((1, ts, GDN_W), lambda bb, i: (bb, i, 0)),
        out_shape=jax.ShapeDtypeStruct((b, s, GDN_W), BF16),
        scratch_shapes=[pltpu.VMEM((GDN_HEADS, GDN_DK, GDN_DV), F32)],
        compiler_params=pltpu.CompilerParams(dimension_semantics=("parallel", "arbitrary"),
                                             vmem_limit_bytes=VMEM_LIMIT),
        name="gdn",
    )(g3, g3, g3, g3, g3, g3, g3, s3, conv_w, ab, nw)


def _diff_kernel(lam_ref, q_ref, k_ref, v_ref, nw_ref, o_ref, *, tq, out_scale):
    i = pl.program_id(2)
    q = q_ref[0]
    lane = lax.broadcasted_iota(jnp.int32, (tq, LANES), 1)
    qs = q * (DIFF_DQK ** -0.5)
    zero = jnp.zeros_like(qs)
    qq = jnp.concatenate([jnp.where(lane < DIFF_DQK, qs, zero),
                          jnp.where(lane >= DIFF_DQK, qs, zero)], axis=0)

    def step(j, carry, masked):
        m, l, acc = carry
        start = pl.multiple_of(j * tq, tq)
        kj = k_ref[0, pl.ds(start, tq), :]
        vj = v_ref[0, pl.ds(start, tq), :]
        s = _dot_nt(qq, kj)
        if masked:
            rr = lax.broadcasted_iota(jnp.int32, (2 * tq, tq), 0)
            cc = lax.broadcasted_iota(jnp.int32, (2 * tq, tq), 1)
            rr = jnp.where(rr >= tq, rr - tq, rr)
            s = jnp.where(cc <= rr, s, NEG_BIG)
        m_new = jnp.maximum(m, jnp.max(s, axis=-1, keepdims=True))
        alpha = jnp.exp(m - m_new)
        p = jnp.exp(s - m_new)
        l = alpha * l + jnp.sum(p, axis=-1, keepdims=True)
        acc = alpha * acc + _dot(p.astype(BF16), vj)
        return m_new, l, acc

    carry = (jnp.full((2 * tq, 1), NEG_BIG, F32), jnp.zeros((2 * tq, 1), F32),
             jnp.zeros((2 * tq, DIFF_DV), F32))
    carry = lax.fori_loop(0, i, lambda j, cr: step(j, cr, False), carry)
    m, l, acc = step(i, carry, True)
    o = acc / l
    od = o[:tq] - lam_ref[0] * o[tq:]
    od = od * lax.rsqrt(jnp.mean(od * od, axis=-1, keepdims=True) + RMS_EPS) * nw_ref[...] * out_scale
    o_ref[0] = od.astype(o_ref.dtype)


def _diff_attn(lam, d3, nw, tq, out_scale):
    b, s, _ = d3.shape
    nh = DIFF_HEADS
    return pl.pallas_call(
        functools.partial(_diff_kernel, tq=tq, out_scale=out_scale),
        grid=(b, nh, s // tq),
        in_specs=[pl.BlockSpec(memory_space=pltpu.SMEM),
                  pl.BlockSpec((1, tq, LANES), lambda bb, h, i: (bb, i, h)),
                  pl.BlockSpec((1, s, LANES), lambda bb, h, i: (bb, 0, nh + h)),
                  pl.BlockSpec((1, s, LANES), lambda bb, h, i: (bb, 0, 2 * nh + h)),
                  pl.BlockSpec(nw.shape, lambda bb, h, i: (0, 0))],
        out_specs=pl.BlockSpec((1, tq, LANES), lambda bb, h, i: (bb, i, h)),
        out_shape=jax.ShapeDtypeStruct((b, s, nh * DIFF_DV), BF16),
        compiler_params=pltpu.CompilerParams(dimension_semantics=("parallel", "parallel", "arbitrary"),
                                             vmem_limit_bytes=VMEM_LIMIT),
        name="diff_attn",
    )(lam, d3, d3, d3, nw)


def _memkv_kernel(mem_ref, wk_ref, wv_ref, kt_ref, v_ref, *, scale):
    mb = mem_ref[0].astype(BF16)
    k = _dot(mb, wk_ref[...])
    kt_ref[0] = (k.T * scale).astype(BF16)
    v_ref[0] = _dot(mb, wv_ref[...]).astype(BF16)


def _memkv(mem, wk, wv, scale):
    b, m, d = mem.shape
    return pl.pallas_call(
        functools.partial(_memkv_kernel, scale=scale),
        grid=(b,),
        in_specs=[pl.BlockSpec((1, m, d), lambda bb: (bb, 0, 0)),
                  pl.BlockSpec(wk.shape, lambda bb: (0, 0)),
                  pl.BlockSpec(wv.shape, lambda bb: (0, 0))],
        out_specs=[pl.BlockSpec((1, d, m), lambda bb: (bb, 0, 0)),
                   pl.BlockSpec((1, m, d), lambda bb: (bb, 0, 0))],
        out_shape=[jax.ShapeDtypeStruct((b, d, m), BF16), jax.ShapeDtypeStruct((b, m, d), BF16)],
        compiler_params=pltpu.CompilerParams(dimension_semantics=("parallel",),
                                             vmem_limit_bytes=VMEM_LIMIT),
        name="memkv",
    )(mem, wk, wv)


def _mid_kernel(x_ref, og_ref, od_ref, wo1_ref, wo2_ref, g1_ref, b1_ref, wq_ref, kt_ref, v_ref, wo_ref,
                g2_ref, b2_ref, rw_ref, rb_ref, x2_ref, x2b_ref, lg_ref):
    h = _dot(og_ref[...], wo1_ref[...]) + _dot(od_ref[...], wo2_ref[...])
    x1 = _layer_norm(DEEPNORM_ALPHA * x_ref[...] + h, g1_ref[...], b1_ref[...])
    q = _dot(x1.astype(BF16), wq_ref[...]).astype(BF16)
    d = q.shape[-1]
    dh = d // XA_HEADS
    heads = []
    for hh in range(XA_HEADS):
        lo, hi = hh * dh, (hh + 1) * dh
        s = _dot(q[:, lo:hi], kt_ref[0, lo:hi, :])
        s = s - jnp.max(s, axis=-1, keepdims=True)
        p = jnp.exp(s)
        p = p / jnp.sum(p, axis=-1, keepdims=True)
        heads.append(_dot(p.astype(BF16), v_ref[0, :, lo:hi]))
    o = jnp.concatenate(heads, axis=-1).astype(BF16)
    h2 = _dot(o, wo_ref[...])
    x2 = _layer_norm(DEEPNORM_ALPHA * x1 + h2, g2_ref[...], b2_ref[...])
    x2_ref[...] = x2
    x2b_ref[...] = x2.astype(BF16)
    lg_ref[...] = jnp.dot(x2, rw_ref[...], precision=HIGHEST, preferred_element_type=F32) + rb_ref[...]


def _mid(x2d, og, od, wo1, wo2, g1, b1, wq, kt, v, wo, g2, b2, rw, rb, tm, rows_per_batch):
    t, d = x2d.shape
    blocks_per_batch = rows_per_batch // tm
    row = lambda w: pl.BlockSpec((tm, w), lambda i: (i, 0))
    full = lambda a: pl.BlockSpec(a.shape, lambda i: (0, 0))
    per_batch = lambda a: pl.BlockSpec((1,) + a.shape[1:], lambda i: (i // blocks_per_batch, 0, 0))
    return pl.pallas_call(
        _mid_kernel,
        grid=(t // tm,),
        in_specs=[row(d), row(og.shape[1]), row(od.shape[1]), full(wo1), full(wo2), full(g1), full(b1),
                  full(wq), per_batch(kt), per_batch(v), full(wo), full(g2), full(b2), full(rw), full(rb)],
        out_specs=[row(d), row(d), row(rw.shape[1])],
        out_shape=[jax.ShapeDtypeStruct((t, d), F32), jax.ShapeDtypeStruct((t, d), BF16),
                   jax.ShapeDtypeStruct((t, rw.shape[1]), F32)],
        compiler_params=pltpu.CompilerParams(dimension_semantics=("parallel",),
                                             vmem_limit_bytes=VMEM_LIMIT),
        name="mid",
    )(x2d, og, od, wo1, wo2, g1, b1, wq, kt, v, wo, g2, b2, rw, rb)


def _moe_kernel(be_ref, nu_ref, x_ref, gt_ref, wg_ref, bg_ref, wu_ref, bu_ref, wd_ref, bd_ref, y_ref,
                wgb_ref, wub_ref, wdb_ref):
    i = pl.program_id(0)
    prev = be_ref[jnp.maximum(i - 1, 0)]
    changed = jnp.logical_or(i == 0, be_ref[i] != prev)

    @pl.when(changed)
    def _():
        wgb_ref[...] = wg_ref[0].astype(BF16)
        wub_ref[...] = wu_ref[0].astype(BF16)
        wdb_ref[...] = wd_ref[0].astype(BF16)

    @pl.when(i < nu_ref[0])
    def _():
        x = x_ref[...]
        hg = _dot(x, wgb_ref[...]) + bg_ref[0]
        hl = _dot(x, wub_ref[...]) + bu_ref[0]
        hg = jnp.minimum(hg, SWIGLU_LIMIT)
        hl = jnp.clip(hl, -SWIGLU_LIMIT, SWIGLU_LIMIT)
        act = hg * _sigmoid(SWIGLU_ALPHA * hg) * (hl + 1.0)
        y = _dot(act.astype(BF16), wdb_ref[...]) + bd_ref[0]
        y_ref[...] = y * gt_ref[...]

    @pl.when(i >= nu_ref[0])
    def _():
        y_ref[...] = jnp.zeros_like(y_ref)


def _moe(block_e, n_used, xs, gate_rows, wg, bg, wu, bu, wd, bd, tm):
    p, d = xs.shape
    dff = wg.shape[2]
    n_blocks = p // tm
    wspec = lambda a: pl.BlockSpec((1,) + a.shape[1:], lambda i, be, nu: (be[i], 0, 0))
    grid_spec = pltpu.PrefetchScalarGridSpec(
        num_scalar_prefetch=2,
        grid=(n_blocks,),
        in_specs=[pl.BlockSpec((tm, d), lambda i, be, nu: (i, 0)),
                  pl.BlockSpec((tm, 1), lambda i, be, nu: (i, 0)),
                  wspec(wg), wspec(bg), wspec(wu), wspec(bu), wspec(wd), wspec(bd)],
        out_specs=pl.BlockSpec((tm, d), lambda i, be, nu: (i, 0)),
        scratch_shapes=[pltpu.VMEM((d, dff), BF16), pltpu.VMEM((d, dff), BF16), pltpu.VMEM((dff, d), BF16)],
    )
    return pl.pallas_call(
        _moe_kernel,
        grid_spec=grid_spec,
        out_shape=jax.ShapeDtypeStruct((p, d), F32),
        compiler_params=pltpu.CompilerParams(dimension_semantics=("arbitrary",),
                                             vmem_limit_bytes=VMEM_LIMIT),
        name="moe_ffn",
    )(block_e, n_used, xs, gate_rows, wg, bg, wu, bu, wd, bd)


def _final_kernel(x_ref, y_ref, g_ref, b_ref, o_ref):
    o_ref[...] = _layer_norm(DEEPNORM_ALPHA * x_ref[...] + y_ref[...], g_ref[...], b_ref[...])


def _final(x2, y, g, b, tm):
    t, d = x2.shape
    row = pl.BlockSpec((tm, d), lambda i: (i, 0))
    full = lambda a: pl.BlockSpec(a.shape, lambda i: (0, 0))
    return pl.pallas_call(
        _final_kernel,
        grid=(t // tm,),
        in_specs=[row, row, full(g), full(b)],
        out_specs=row,
        out_shape=jax.ShapeDtypeStruct((t, d), F32),
        compiler_params=pltpu.CompilerParams(dimension_semantics=("parallel",),
                                             vmem_limit_bytes=VMEM_LIMIT),
        name="final_ln",
    )(x2, y, g, b)


def _route(logits, tm):
    t = logits.shape[0]
    top_val, top_idx = lax.top_k(logits, TOP_K)
    gate = jax.nn.softmax(top_val, axis=-1)
    a = t * TOP_K
    flat_e = top_idx.reshape(-1)
    order = jnp.argsort(flat_e)
    sorted_e = flat_e[order]
    token_sorted = (order // TOP_K).astype(jnp.int32)
    counts = jnp.bincount(flat_e, length=N_EXPERTS)
    padded = (counts + tm - 1) // tm * tm
    start = jnp.cumsum(counts) - counts
    pad_end = jnp.cumsum(padded)
    pad_start = pad_end - padded
    dest_sorted = (pad_start[sorted_e] + (jnp.arange(a) - start[sorted_e])).astype(jnp.int32)
    n_blocks = a // tm + N_EXPERTS
    p = n_blocks * tm
    src_token = jnp.zeros((p,), jnp.int32).at[dest_sorted].set(token_sorted)
    gate_rows = jnp.zeros((p,), F32).at[dest_sorted].set(gate.reshape(-1)[order])
    dest_assign = jnp.zeros((a,), jnp.int32).at[order].set(dest_sorted)
    block_e = jnp.minimum(jnp.searchsorted(pad_end, jnp.arange(n_blocks) * tm, side="right"),
                          N_EXPERTS - 1).astype(jnp.int32)
    n_used = (pad_end[-1] // tm).astype(jnp.int32).reshape(1)
    return src_token, gate_rows.reshape(p, 1), dest_assign.reshape(t, TOP_K), block_e, n_used


def kernel(x, mem, w_in, gdn_conv_w, gdn_a_log, gdn_dt_bias, gdn_norm_w, diff_lq1, diff_lk1, diff_lq2, diff_lk2,
           diff_norm_w, w_out, ln1_g, ln1_b, xa_wq, xa_wk, xa_wv, xa_wo, ln2_g, ln2_b, router_w, router_b,
           exp_w_gate, exp_b_gate, exp_w_up, exp_b_up, exp_w_down, exp_b_down, ln3_g, ln3_b):
    b, s, d = x.shape
    t = b * s
    x2d = x.reshape(t, d)
    for l in range(DEPTH):
        lambda_init = 0.8 - 0.6 * math.exp(-0.3 * l)
        n_g = 4 * GDN_W
        w = w_in[l]
        wg = w[:, :n_g].astype(BF16)
        small = w[:, n_g:n_g + 2 * GDN_HEADS]
        ws = jnp.pad(small, ((0, 0), (0, SMALL_W - 2 * GDN_HEADS))).astype(BF16)
        wd = w[:, n_g + 2 * GDN_HEADS:].astype(BF16)
        conv_w = gdn_conv_w[l].reshape(GDN_CONV, 3 * GDN_W).astype(F32)
        ab = jnp.zeros((2, SMALL_W), F32)
        ab = ab.at[0, GDN_HEADS:2 * GDN_HEADS].set(-jnp.exp(gdn_a_log[l].astype(F32)))
        ab = ab.at[1, GDN_HEADS:2 * GDN_HEADS].set(gdn_dt_bias[l].astype(F32))
        lam = (jnp.exp(jnp.sum(diff_lq1[l].astype(F32) * diff_lk1[l].astype(F32)))
               - jnp.exp(jnp.sum(diff_lq2[l].astype(F32) * diff_lk2[l].astype(F32))) + lambda_init).reshape(1)

        g2d, d2d, s2d = _inproj(x2d, wg, wd, ws, tm=512)
        o_gdn = _gdn(g2d.reshape(b, s, n_g), s2d.reshape(b, s, SMALL_W), conv_w, ab,
                     gdn_norm_w[l].reshape(1, GDN_DV).astype(F32), ts=256)
        o_diff = _diff_attn(lam, d2d.reshape(b, s, 3 * DIFF_W), diff_norm_w[l].reshape(1, DIFF_DV).astype(F32),
                            tq=256, out_scale=1.0 - lambda_init)

        dh = d // XA_HEADS
        kt, v = _memkv(mem, xa_wk[l].astype(BF16), xa_wv[l].astype(BF16), scale=dh ** -0.5)
        wo_mix = w_out[l].astype(BF16)
        rw = jnp.pad(router_w[l].astype(F32), ((0, 0), (0, LANES - N_EXPERTS)))
        rb = jnp.pad(router_b[l].astype(F32), (0, LANES - N_EXPERTS)).reshape(1, LANES)
        row = lambda a: a.reshape(1, -1).astype(F32)
        x2, x2b, logits = _mid(x2d, o_gdn.reshape(t, GDN_W), o_diff.reshape(t, DIFF_HEADS * DIFF_DV),
                               wo_mix[:GDN_W], wo_mix[GDN_W:], row(ln1_g[l]), row(ln1_b[l]),
                               xa_wq[l].astype(BF16), kt, v, xa_wo[l].astype(BF16), row(ln2_g[l]), row(ln2_b[l]),
                               rw, rb, tm=512, rows_per_batch=s)

        tm_moe = 256
        src_token, gate_rows, dest_assign, block_e, n_used = _route(logits[:, :N_EXPERTS], tm_moe)
        xs = jnp.take(x2b, src_token, axis=0)
        ys = _moe(block_e, n_used, xs, gate_rows,
                  exp_w_gate[l], exp_b_gate[l].reshape(N_EXPERTS, 1, -1), exp_w_up[l],
                  exp_b_up[l].reshape(N_EXPERTS, 1, -1), exp_w_down[l], exp_b_down[l].reshape(N_EXPERTS, 1, -1),
                  tm=tm_moe)
        y = jnp.sum(jnp.take(ys, dest_assign.reshape(-1), axis=0).reshape(t, TOP_K, d), axis=1)
        x2d = _final(x2, y, row(ln3_g[l]), row(ln3_b[l]), tm=512)
    return x2d.reshape(b, s, d)
```

```python
import functools
import math

import jax
import jax.numpy as jnp
from jax import lax
from jax.experimental import pallas as pl
from jax.experimental.pallas import tpu as pltpu
from jax.experimental.pallas import tpu_sc as plsc

F32 = jnp.float32
BF16 = jnp.bfloat16
U32 = jnp.uint32
I32 = jnp.int32
HIGHEST = lax.Precision.HIGHEST

GDN_HEADS = 4
GDN_DK = 128
GDN_DV = 128
GDN_CONV = 4
GDN_CHUNK = 64
DIFF_HEADS = 4
DIFF_DQK = 64
DIFF_DV = 128
XA_HEADS = 4
N_EXPERTS = 32
TOP_K = 4
SWIGLU_ALPHA = 1.702
SWIGLU_LIMIT = 7.0
LN_EPS = 1e-5
RMS_EPS = 1e-6
DEPTH = 1
DEEPNORM_ALPHA = (2 * DEPTH) ** 0.25

GDN_W = GDN_HEADS * GDN_DK
DIFF_W = DIFF_HEADS * 2 * DIFF_DQK
SMALL_W = 128

LANES = 128
SUBLANES = 8
BF16_SUBLANES = 16
VMEM_LIMIT = 56 * 1024 * 1024
SC_WINDOW = 128
SC_ROW_WORDS = 256

NEG_BIG = -1e30
LOG2E = 1.4426950408889634


def _dot(a, b):
    return jnp.dot(a, b, preferred_element_type=F32)


def _dot_nt(a, b):
    return lax.dot_general(a, b, (((1,), (1,)), ((), ())), preferred_element_type=F32)


def _dot_tn(a, b):
    return lax.dot_general(a, b, (((0,), (0,)), ((), ())), preferred_element_type=F32)


def _layer_norm(y, g, b):
    mu = jnp.mean(y, axis=-1, keepdims=True)
    d = y - mu
    var = jnp.mean(d * d, axis=-1, keepdims=True)
    return d * lax.rsqrt(var + LN_EPS) * g + b


def _sigmoid(x):
    return 1.0 / (1.0 + jnp.exp(-x))


def _pack_bf16_pairs(x):
    n = x.shape[1] // 2
    bits = pltpu.bitcast(x.astype(BF16).astype(F32), U32)
    return (bits[:, :n] >> 16) | bits[:, n:]


def _unpack_bf16_pairs(w):
    lo = pltpu.bitcast(w << 16, F32)
    hi = pltpu.bitcast(w & jnp.uint32(0xFFFF0000), F32)
    return lo, hi


def _inproj_kernel(x_ref, wg_ref, wq_ref, wk_ref, wv_ref, ws_ref, g_ref, qt_ref, k_ref, vt_ref, s_ref, *, q_scale):
    xb = x_ref[...].astype(BF16)
    g_ref[...] = _dot(xb, wg_ref[...]).astype(BF16)
    qt_ref[0] = (_dot(xb, wq_ref[...]) * q_scale).T.astype(BF16)
    k_ref[...] = _dot(xb, wk_ref[...]).astype(BF16)
    vt_ref[0] = _dot(xb, wv_ref[...]).T.astype(BF16)
    s_ref[...] = _dot(xb, ws_ref[...])


def _inproj(x2d, wg, wq, wk, wv, ws, tm, batch, q_scale):
    t, d = x2d.shape
    s = t // batch
    per_batch = s // tm
    full = lambda a: pl.BlockSpec(a.shape, lambda i: (0, 0))
    row = lambda w: pl.BlockSpec((tm, w), lambda i: (i, 0))
    tr = lambda w: pl.BlockSpec((1, w, tm), lambda i: (i // per_batch, 0, i % per_batch))
    return pl.pallas_call(
        functools.partial(_inproj_kernel, q_scale=q_scale),
        grid=(t // tm,),
        in_specs=[row(d), full(wg), full(wq), full(wk), full(wv), full(ws)],
        out_specs=[row(wg.shape[1]), tr(wq.shape[1]), row(wk.shape[1]), tr(wv.shape[1]), row(ws.shape[1])],
        out_shape=[jax.ShapeDtypeStruct((t, wg.shape[1]), BF16),
                   jax.ShapeDtypeStruct((batch, wq.shape[1], s), BF16),
                   jax.ShapeDtypeStruct((t, wk.shape[1]), BF16),
                   jax.ShapeDtypeStruct((batch, wv.shape[1], s), BF16),
                   jax.ShapeDtypeStruct((t, ws.shape[1]), F32)],
        compiler_params=pltpu.CompilerParams(dimension_semantics=("parallel",),
                                             vmem_limit_bytes=VMEM_LIMIT),
        name="inproj",
    )(x2d, wg, wq, wk, wv, ws)


def _gdn_kernel(q_ref, k_ref, v_ref, z_ref, hq_ref, hk_ref, hv_ref, s_ref, cw_ref, ab_ref, nw_ref,
                o_ref, state_ref, *, ts):
    c_len = GDN_CHUNK
    n_chunks = ts // c_len
    nh = GDN_HEADS
    i = pl.program_id(1)

    @pl.when(i == 0)
    def _():
        state_ref[...] = jnp.zeros_like(state_ref)

    row8 = lax.broadcasted_iota(I32, (SUBLANES, GDN_W), 0)
    first = i == 0

    def conv_silu(x_ref_, h_ref_, col0):
        x = x_ref_[0].astype(F32)
        hb = jnp.where(first, 0.0, h_ref_[0].astype(F32))
        w = cw_ref[:, col0:col0 + GDN_W]
        y = x * w[GDN_CONV - 1:GDN_CONV, :]
        for j in range(1, GDN_CONV):
            xr = pltpu.roll(x, j, 0)
            fix = pltpu.roll(hb, j, 0)
            top = jnp.where(row8 < j, fix, xr[0:SUBLANES])
            xr = jnp.concatenate([top, xr[SUBLANES:]], axis=0)
            y = y + xr * w[GDN_CONV - 1 - j:GDN_CONV - j, :]
        return y * _sigmoid(y)

    qa = conv_silu(q_ref, hq_ref, 0)
    ka = conv_silu(k_ref, hk_ref, GDN_W)
    va = conv_silu(v_ref, hv_ref, 2 * GDN_W)

    sg = s_ref[0]
    beta_all = _sigmoid(sg)
    sp_in = sg + ab_ref[1:2, :]
    softplus = jnp.maximum(sp_in, 0.0) + jnp.log(1.0 + jnp.exp(-jnp.abs(sp_in)))
    g_step = ab_ref[0:1, :] * softplus

    r = lax.broadcasted_iota(I32, (ts, ts), 0)
    c = lax.broadcasted_iota(I32, (ts, ts), 1)
    tri = jnp.where((r // c_len) == (c // c_len), jnp.where(c <= r, 1.0, 0.0), 0.0)
    gc = jnp.dot(tri, g_step, precision=HIGHEST, preferred_element_type=F32)
    gct = [gc[cc * c_len:(cc + 1) * c_len, :].T for cc in range(n_chunks)]

    ri = lax.broadcasted_iota(I32, (c_len, c_len), 0)
    ci = lax.broadcasted_iota(I32, (c_len, c_len), 1)
    causal = ci <= ri
    strict = ci < ri
    nw = nw_ref[...]

    pairs = [(h, cc) for h in range(nh) for cc in range(n_chunks)]
    qs, ks, kbs, vbs, kbes, decays, qds, kts, gls = [], [], [], [], [], [], [], [], []
    for h in range(nh):
        lo, hi = h * GDN_DK, (h + 1) * GDN_DK
        qh = qa[:, lo:hi]
        kh = ka[:, lo:hi]
        vh = va[:, lo:hi]
        qh = qh * (lax.rsqrt(jnp.sum(qh * qh, axis=-1, keepdims=True) + RMS_EPS) * (GDN_DK ** -0.5))
        kh = kh * lax.rsqrt(jnp.sum(kh * kh, axis=-1, keepdims=True) + RMS_EPS)
        beta = beta_all[:, h:h + 1]
        gcol_all = gc[:, nh + h:nh + h + 1]
        for cc in range(n_chunks):
            r0, r1 = cc * c_len, (cc + 1) * c_len
            qc, kc, vc = qh[r0:r1], kh[r0:r1], vh[r0:r1]
            bcol = beta[r0:r1]
            gcol = gcol_all[r0:r1]
            grow = gct[cc][nh + h:nh + h + 1, :]
            decay = jnp.where(causal, jnp.exp(jnp.where(causal, gcol - grow, 0.0)), 0.0)
            eg = jnp.exp(gcol)
            g_last = gcol[c_len - 1:c_len, :]
            kb = kc * bcol
            qs.append(qc.astype(BF16))
            ks.append(kc.astype(BF16))
            kbs.append(kb)
            vbs.append(vc * bcol)
            kbes.append(kb * eg)
            decays.append(decay)
            qds.append((qc * eg).astype(BF16))
            kts.append((kc * jnp.exp(g_last - gcol)).astype(BF16))
            gls.append(jnp.exp(g_last))

    n = len(pairs)
    ms = [-jnp.where(strict, _dot_nt(kbs[b].astype(BF16), ks[b]) * decays[b], 0.0) for b in range(n)]
    ais = [jnp.where(causal, _dot_nt(qs[b], ks[b]) * decays[b], 0.0).astype(BF16) for b in range(n)]
    ys = ms
    for _ in range(5):
        mb = [m.astype(BF16) for m in ms]
        ms = [_dot(mb[b], mb[b]) for b in range(n)]
        mb = [m.astype(BF16) for m in ms]
        ys = [ys[b] + ms[b] + _dot(ys[b].astype(BF16), mb[b]) for b in range(n)]
    yb = [y.astype(BF16) for y in ys]
    us = [vbs[b] + _dot(yb[b], vbs[b].astype(BF16)) for b in range(n)]
    ws = [(kbes[b] + _dot(yb[b], kbes[b].astype(BF16))).astype(BF16) for b in range(n)]

    states = [state_ref[h] for h in range(nh)]
    outs = [[None] * n_chunks for _ in range(nh)]
    for cc in range(n_chunks):
        idx = [h * n_chunks + cc for h in range(nh)]
        sb = [st.astype(BF16) for st in states]
        v_new = [us[idx[h]] - _dot(ws[idx[h]], sb[h]) for h in range(nh)]
        vnb = [v.astype(BF16) for v in v_new]
        for h in range(nh):
            outs[h][cc] = _dot(qds[idx[h]], sb[h]) + _dot(ais[idx[h]], vnb[h])
        states = [states[h] * gls[idx[h]] + _dot_tn(kts[idx[h]], vnb[h]) for h in range(nh)]
    for h in range(nh):
        lo, hi = h * GDN_DK, (h + 1) * GDN_DK
        state_ref[h] = states[h]
        o = jnp.concatenate(outs[h], axis=0)
        o = o * lax.rsqrt(jnp.mean(o * o, axis=-1, keepdims=True) + RMS_EPS) * nw
        z = z_ref[0, :, lo:hi].astype(F32)
        o_ref[0, :, lo:hi] = (o * (z * _sigmoid(z))).astype(o_ref.dtype)


def _gdn(g3, s3, conv_w, ab, nw, ts):
    b, s, _ = g3.shape
    per_block = ts // SUBLANES

    def blk(col):
        return pl.BlockSpec((1, ts, GDN_W), lambda bb, i: (bb, i, col))

    def halo(col):
        return pl.BlockSpec((1, SUBLANES, GDN_W),
                            lambda bb, i: (bb, jnp.maximum(i * per_block - 1, 0), col))

    return pl.pallas_call(
        functools.partial(_gdn_kernel, ts=ts),
        grid=(b, s // ts),
        in_specs=[blk(0), blk(1), blk(2), blk(3), halo(0), halo(1), halo(2),
                  pl.BlockSpec((1, ts, SMALL_W), lambda bb, i: (bb, i, 0)),
                  pl.BlockSpec(conv_w.shape, lambda bb, i: (0, 0)),
                  pl.BlockSpec(ab.shape, lambda bb, i: (0, 0)),
                  pl.BlockSpec(nw.shape, lambda bb, i: (0, 0))],
        out_specs=pl.BlockSpec((1, ts, GDN_W), lambda bb, i: (bb, i, 0)),
        out_shape=jax.ShapeDtypeStruct((b, s, GDN_W), BF16),
        scratch_shapes=[pltpu.VMEM((GDN_HEADS, GDN_DK, GDN_DV), F32)],
        compiler_params=pltpu.CompilerParams(dimension_semantics=("parallel", "arbitrary"),
                                             vmem_limit_bytes=VMEM_LIMIT),
        name="gdn",
    )(g3, g3, g3, g3, g3, g3, g3, s3, conv_w, ab, nw)


def _diff_kernel(lam_ref, qt_ref, k_ref, vt_ref, nw_ref, o_ref, m_ref, acc_ref, s_ref, cm_ref, *, tq, out_scale):
    i = pl.program_id(2)
    qt = qt_ref[0]
    rowi = lax.broadcasted_iota(I32, (2 * DIFF_DQK, tq), 0)
    zero = jnp.zeros_like(qt)
    qq = jnp.concatenate([jnp.where(rowi < DIFF_DQK, qt, zero),
                          jnp.where(rowi >= DIFF_DQK, qt, zero)], axis=1)
    m_ref[...] = jnp.full(m_ref.shape, NEG_BIG, F32)
    acc_ref[...] = jnp.zeros(acc_ref.shape, F32)
    ones_rows = jnp.ones((BF16_SUBLANES, tq), BF16)

    def scores(j, slot, masked):
        start = pl.multiple_of(j * tq, tq)
        s = _dot(k_ref[0, pl.ds(start, tq), :], qq)
        if masked:
            kr = lax.broadcasted_iota(I32, (tq, 2 * tq), 0)
            qc = lax.broadcasted_iota(I32, (tq, 2 * tq), 1)
            qc = jnp.where(qc >= tq, qc - tq, qc)
            s = jnp.where(kr <= qc, s, NEG_BIG)
        s_ref[slot] = s
        cm_ref[slot] = jnp.max(s, axis=0, keepdims=True)

    def update(j, slot):
        start = pl.multiple_of(j * tq, tq)
        vtj = vt_ref[0, :, pl.ds(start, tq)]
        m_old = m_ref[...]
        m_new = jnp.maximum(m_old, cm_ref[slot])
        alpha = jnp.exp2(m_old - m_new)
        p = jnp.exp2(s_ref[slot] - m_new).astype(BF16)
        vt_ext = jnp.concatenate([vtj, ones_rows], axis=0)
        acc_ref[...] = alpha * acc_ref[...] + _dot(vt_ext, p)
        m_ref[...] = m_new

    scores(i, 0, True)

    def pair(jj, carry):
        held = jnp.where(jj == 0, i, 2 * jj - 1)
        scores(2 * jj, 1, False)
        update(held, 0)
        scores(2 * jj + 1, 0, False)
        update(2 * jj, 1)
        return carry

    n_pairs = i // 2
    lax.fori_loop(0, n_pairs, pair, 0)
    held = jnp.where(n_pairs == 0, i, 2 * n_pairs - 1)

    @pl.when(i % 2 == 1)
    def _():
        scores(i - 1, 1, False)
        update(held, 0)
        update(i - 1, 1)

    @pl.when(i % 2 == 0)
    def _():
        update(held, 0)

    acc = acc_ref[...]
    o = acc[:DIFF_DV] / acc[DIFF_DV:DIFF_DV + 1]
    od = o[:, :tq] - lam_ref[0] * o[:, tq:]
    od = od * lax.rsqrt(jnp.mean(od * od, axis=0, keepdims=True) + RMS_EPS) * nw_ref[...] * out_scale
    o_ref[0] = od.T.astype(o_ref.dtype)


def _diff_attn(lam, qt, k3, vt, nw_col, tq, out_scale):
    b, s, _ = k3.shape
    nh = DIFF_HEADS
    return pl.pallas_call(
        functools.partial(_diff_kernel, tq=tq, out_scale=out_scale),
        grid=(b, nh, s // tq),
        in_specs=[pl.BlockSpec(memory_space=pltpu.SMEM),
                  pl.BlockSpec((1, LANES, tq), lambda bb, h, i: (bb, h, i)),
                  pl.BlockSpec((1, s, LANES), lambda bb, h, i: (bb, 0, h)),
                  pl.BlockSpec((1, LANES, s), lambda bb, h, i: (bb, h, 0)),
                  pl.BlockSpec(nw_col.shape, lambda bb, h, i: (0, 0))],
        out_specs=pl.BlockSpec((1, tq, LANES), lambda bb, h, i: (bb, i, h)),
        out_shape=jax.ShapeDtypeStruct((b, s, nh * DIFF_DV), BF16),
        scratch_shapes=[pltpu.VMEM((1, 2 * tq), F32), pltpu.VMEM((DIFF_DV + BF16_SUBLANES, 2 * tq), F32),
                        pltpu.VMEM((2, tq, 2 * tq), F32), pltpu.VMEM((2, 1, 2 * tq), F32)],
        compiler_params=pltpu.CompilerParams(dimension_semantics=("parallel", "parallel", "arbitrary"),
                                             vmem_limit_bytes=VMEM_LIMIT),
        name="diff_attn",
    )(lam, qt, k3, vt, nw_col)


def _memkv_kernel(mem_ref, wk_ref, wv_ref, kt_ref, v_ref, *, scale):
    mb = mem_ref[0].astype(BF16)
    k = _dot(mb, wk_ref[...])
    kt_ref[0] = (k.T * scale).astype(BF16)
    v_ref[0] = _dot(mb, wv_ref[...]).astype(BF16)


def _memkv(mem, wk, wv, scale):
    b, m, d = mem.shape
    return pl.pallas_call(
        functools.partial(_memkv_kernel, scale=scale),
        grid=(b,),
        in_specs=[pl.BlockSpec((1, m, d), lambda bb: (bb, 0, 0)),
                  pl.BlockSpec(wk.shape, lambda bb: (0, 0)),
                  pl.BlockSpec(wv.shape, lambda bb: (0, 0))],
        out_specs=[pl.BlockSpec((1, d, m), lambda bb: (bb, 0, 0)),
                   pl.BlockSpec((1, m, d), lambda bb: (bb, 0, 0))],
        out_shape=[jax.ShapeDtypeStruct((b, d, m), BF16), jax.ShapeDtypeStruct((b, m, d), BF16)],
        compiler_params=pltpu.CompilerParams(dimension_semantics=("parallel",),
                                             vmem_limit_bytes=VMEM_LIMIT),
        name="memkv",
    )(mem, wk, wv)


def _mid_kernel(x_ref, og_ref, od_ref, wo1_ref, wo2_ref, g1_ref, b1_ref, wq_ref, kt_ref, v_ref, wo_ref,
                g2_ref, b2_ref, rw_ref, rb_ref, x2_ref, x2p_ref, route_ref, gate_ref, cnt_ref, run_ref):
    i = pl.program_id(0)

    @pl.when(i == 0)
    def _():
        run_ref[...] = jnp.zeros_like(run_ref)

    h = _dot(og_ref[...], wo1_ref[...]) + _dot(od_ref[...], wo2_ref[...])
    x1 = _layer_norm(DEEPNORM_ALPHA * x_ref[...] + h, g1_ref[...], b1_ref[...])
    q = _dot(x1.astype(BF16), wq_ref[...]).astype(BF16)
    d = q.shape[-1]
    dh = d // XA_HEADS
    heads = []
    for hh in range(XA_HEADS):
        lo, hi = hh * dh, (hh + 1) * dh
        s = _dot(q[:, lo:hi], kt_ref[0, lo:hi, :])
        s = s - jnp.max(s, axis=-1, keepdims=True)
        p = jnp.exp(s)
        p = p / jnp.sum(p, axis=-1, keepdims=True)
        heads.append(_dot(p.astype(BF16), v_ref[0, :, lo:hi]))
    o = jnp.concatenate(heads, axis=-1).astype(BF16)
    h2 = _dot(o, wo_ref[...])
    x2 = _layer_norm(DEEPNORM_ALPHA * x1 + h2, g2_ref[...], b2_ref[...])
    x2_ref[...] = x2
    x2p_ref[...] = _pack_bf16_pairs(x2)

    tm = x2.shape[0]
    logits = jnp.dot(x2, rw_ref[...], precision=HIGHEST, preferred_element_type=F32) + rb_ref[...]
    lane = lax.broadcasted_iota(I32, (tm, LANES), 1)
    lg = jnp.where(lane < N_EXPERTS, logits, NEG_BIG)
    sel = jnp.zeros((tm, LANES), F32)
    vals, hots, idxs = [], [], []
    for _ in range(TOP_K):
        mx = jnp.max(lg, axis=-1, keepdims=True)
        idx = jnp.min(jnp.where(lg == mx, lane, LANES), axis=-1, keepdims=True)
        hot = lane == idx
        vals.append(mx)
        idxs.append(idx)
        hots.append(hot)
        lg = jnp.where(hot, NEG_BIG, lg)
        sel = sel + jnp.where(hot, 1.0, 0.0)
    ex = [jnp.exp(v - vals[0]) for v in vals]
    den = ex[0] + ex[1] + ex[2] + ex[3]
    rr = lax.broadcasted_iota(I32, (tm, tm), 0)
    cc = lax.broadcasted_iota(I32, (tm, tm), 1)
    tri = jnp.where(cc < rr, 1.0, 0.0).astype(BF16)
    before = _dot(tri, sel.astype(BF16)) + run_ref[...]
    route = jnp.zeros((tm, LANES), F32)
    gates = jnp.zeros((tm, LANES), F32)
    for kk in range(TOP_K):
        rank = jnp.sum(jnp.where(hots[kk], before, 0.0), axis=-1, keepdims=True)
        route = jnp.where(lane == kk, idxs[kk].astype(F32), route)
        route = jnp.where(lane == TOP_K + kk, rank, route)
        gates = jnp.where(lane == kk, ex[kk] / den, gates)
    route_ref[...] = route.astype(I32)
    gate_ref[...] = gates
    run = run_ref[...] + jnp.sum(sel, axis=0, keepdims=True)
    run_ref[...] = run
    cnt_ref[...] = run.astype(I32)


def _mid(x2d, og, od, wo1, wo2, g1, b1, wq, kt, v, wo, g2, b2, rw, rb, tm, rows_per_batch):
    t, d = x2d.shape
    blocks_per_batch = rows_per_batch // tm
    row = lambda w: pl.BlockSpec((tm, w), lambda i: (i, 0))
    full = lambda a: pl.BlockSpec(a.shape, lambda i: (0, 0))
    per_batch = lambda a: pl.BlockSpec((1,) + a.shape[1:], lambda i: (i // blocks_per_batch, 0, 0))
    return pl.pallas_call(
        _mid_kernel,
        grid=(t // tm,),
        in_specs=[row(d), row(og.shape[1]), row(od.shape[1]), full(wo1), full(wo2), full(g1), full(b1),
                  full(wq), per_batch(kt), per_batch(v), full(wo), full(g2), full(b2), full(rw), full(rb)],
        out_specs=[row(d), row(d // 2), row(LANES), row(LANES), pl.BlockSpec((1, LANES), lambda i: (0, 0))],
        out_shape=[jax.ShapeDtypeStruct((t, d), F32), jax.ShapeDtypeStruct((t, d // 2), U32),
                   jax.ShapeDtypeStruct((t, LANES), I32), jax.ShapeDtypeStruct((t, LANES), F32),
                   jax.ShapeDtypeStruct((1, LANES), I32)],
        scratch_shapes=[pltpu.VMEM((1, LANES), F32)],
        compiler_params=pltpu.CompilerParams(dimension_semantics=("arbitrary",),
                                             vmem_limit_bytes=VMEM_LIMIT),
        name="mid",
    )(x2d, og, od, wo1, wo2, g1, b1, wq, kt, v, wo, g2, b2, rw, rb)


def _sc_scatter_rows(rows, idx4, n_out):
    t, d = rows.shape
    mesh = plsc.VectorSubcoreMesh(core_axis_name="c", subcore_axis_name="s")

    @functools.partial(pl.kernel, out_type=jax.ShapeDtypeStruct((n_out, d), rows.dtype), mesh=mesh)
    def scatter_kernel(x_hbm, i0_hbm, i1_hbm, i2_hbm, i3_hbm, o_hbm):
        def body(x_vmem, i0, i1, i2, i3):
            for iv in (i0, i1, i2, i3):
                pltpu.sync_copy(x_vmem, o_hbm.at[iv.at[0]])

        ispec = pl.BlockSpec((1, SC_WINDOW), lambda i: (0, i))
        pltpu.emit_pipeline(
            body,
            grid=(t // SC_WINDOW,),
            in_specs=[pl.BlockSpec((SC_WINDOW, d), lambda i: (i, 0)), ispec, ispec, ispec, ispec],
            out_specs=[],
            core_axis_name=("c", "s"),
            dimension_semantics=(pltpu.PARALLEL,),
        )(x_hbm, i0_hbm, i1_hbm, i2_hbm, i3_hbm)

    return scatter_kernel(rows, *[idx4[k:k + 1] for k in range(TOP_K)])


def _sc_gather_rows(table, idx):
    n = idx.shape[0]
    d = table.shape[1]
    mesh = plsc.VectorSubcoreMesh(core_axis_name="c", subcore_axis_name="s")

    @functools.partial(pl.kernel, out_type=jax.ShapeDtypeStruct((n, d), table.dtype), mesh=mesh)
    def gather_kernel(x_hbm, i_hbm, o_hbm):
        def body(i_vmem, o_vmem):
            pltpu.sync_copy(x_hbm.at[i_vmem.at[0]], o_vmem)

        pltpu.emit_pipeline(
            body,
            grid=(n // SC_WINDOW,),
            in_specs=[pl.BlockSpec((1, SC_WINDOW), lambda i: (0, i))],
            out_specs=[pl.BlockSpec((SC_WINDOW, d), lambda i: (i, 0))],
            core_axis_name=("c", "s"),
            dimension_semantics=(pltpu.PARALLEL,),
        )(i_hbm, o_hbm)

    return gather_kernel(table, idx.reshape(1, n))


def _moe_kernel(be_ref, nu_ref, x_ref, wg_ref, bg_ref, wu_ref, bu_ref, wd_ref, bd_ref, y_ref,
                wgb_ref, wub_ref, wdb_ref):
    i = pl.program_id(0)
    prev = be_ref[jnp.maximum(i - 1, 0)]
    changed = jnp.logical_or(i == 0, be_ref[i] != prev)

    @pl.when(changed)
    def _():
        wgb_ref[...] = wg_ref[0].astype(BF16)
        wub_ref[...] = wu_ref[0].astype(BF16)
        wdb_ref[...] = wd_ref[0].astype(BF16)

    @pl.when(i < nu_ref[0])
    def _():
        lo, hi = _unpack_bf16_pairs(x_ref[...])
        x = jnp.concatenate([lo, hi], axis=-1).astype(BF16)
        hg = _dot(x, wgb_ref[...]) + bg_ref[0]
        hl = _dot(x, wub_ref[...]) + bu_ref[0]
        hg = jnp.minimum(hg, SWIGLU_LIMIT)
        hl = jnp.clip(hl, -SWIGLU_LIMIT, SWIGLU_LIMIT)
        act = hg * _sigmoid(SWIGLU_ALPHA * hg) * (hl + 1.0)
        y = _dot(act.astype(BF16), wdb_ref[...]) + bd_ref[0]
        y_ref[...] = _pack_bf16_pairs(y)

    @pl.when(i >= nu_ref[0])
    def _():
        y_ref[...] = jnp.zeros_like(y_ref)


def _moe(block_e, n_used, xs, wg, bg, wu, bu, wd, bd, tm):
    p, dp = xs.shape
    d, dff = wg.shape[1], wg.shape[2]
    n_blocks = p // tm
    wspec = lambda a: pl.BlockSpec((1,) + a.shape[1:], lambda i, be, nu: (be[i], 0, 0))
    grid_spec = pltpu.PrefetchScalarGridSpec(
        num_scalar_prefetch=2,
        grid=(n_blocks,),
        in_specs=[pl.BlockSpec((tm, dp), lambda i, be, nu: (i, 0)),
                  wspec(wg), wspec(bg), wspec(wu), wspec(bu), wspec(wd), wspec(bd)],
        out_specs=pl.BlockSpec((tm, dp), lambda i, be, nu: (i, 0)),
        scratch_shapes=[pltpu.VMEM((d, dff), BF16), pltpu.VMEM((d, dff), BF16), pltpu.VMEM((dff, d), BF16)],
    )
    return pl.pallas_call(
        _moe_kernel,
        grid_spec=grid_spec,
        out_shape=jax.ShapeDtypeStruct((p, dp), U32),
        compiler_params=pltpu.CompilerParams(dimension_semantics=("arbitrary",),
                                             vmem_limit_bytes=VMEM_LIMIT),
        name="moe_ffn",
    )(block_e, n_used, xs, wg, bg, wu, bu, wd, bd)


def _final_kernel(x_ref, y4_ref, gate_ref, g_ref, b_ref, o_ref):
    half = x_ref.shape[1] // 2
    gates = gate_ref[...]
    y_lo = jnp.zeros((x_ref.shape[0], half), F32)
    y_hi = jnp.zeros((x_ref.shape[0], half), F32)
    for kk in range(TOP_K):
        lo, hi = _unpack_bf16_pairs(y4_ref[:, kk * half:(kk + 1) * half])
        gk = gates[:, kk:kk + 1]
        y_lo = y_lo + gk * lo
        y_hi = y_hi + gk * hi
    y = jnp.concatenate([y_lo, y_hi], axis=-1)
    o_ref[...] = _layer_norm(DEEPNORM_ALPHA * x_ref[...] + y, g_ref[...], b_ref[...])


def _final(x2, y4, gates, g, b, tm):
    t, d = x2.shape
    row = lambda w: pl.BlockSpec((tm, w), lambda i: (i, 0))
    full = lambda a: pl.BlockSpec(a.shape, lambda i: (0, 0))
    return pl.pallas_call(
        _final_kernel,
        grid=(t // tm,),
        in_specs=[row(d), row(y4.shape[1]), row(gates.shape[1]), full(g), full(b)],
        out_specs=row(d),
        out_shape=jax.ShapeDtypeStruct((t, d), F32),
        compiler_params=pltpu.CompilerParams(dimension_semantics=("parallel",),
                                             vmem_limit_bytes=VMEM_LIMIT),
        name="final_ln",
    )(x2, y4, gates, g, b)


def _layout(route, counts, tm):
    t = route.shape[0]
    experts = route[:, :TOP_K]
    ranks = route[:, TOP_K:2 * TOP_K]
    counts = counts[0, :N_EXPERTS]
    padded = (counts + tm - 1) // tm * tm
    pad_end = jnp.cumsum(padded)
    pad_start = pad_end - padded
    eids = jnp.arange(N_EXPERTS, dtype=I32)
    dest = ranks + jnp.sum(jnp.where(experts[:, :, None] == eids, pad_start, 0), axis=-1)
    n_blocks = t * TOP_K // tm + N_EXPERTS
    starts = jnp.arange(n_blocks, dtype=I32) * tm
    block_e = jnp.minimum(jnp.sum((pad_end[None, :] <= starts[:, None]).astype(I32), axis=-1), N_EXPERTS - 1)
    n_used = (pad_end[-1] // tm).astype(I32).reshape(1)
    return dest.astype(I32), block_e.astype(I32), n_used, n_blocks


def kernel(x, mem, w_in, gdn_conv_w, gdn_a_log, gdn_dt_bias, gdn_norm_w, diff_lq1, diff_lk1, diff_lq2, diff_lk2,
           diff_norm_w, w_out, ln1_g, ln1_b, xa_wq, xa_wk, xa_wv, xa_wo, ln2_g, ln2_b, router_w, router_b,
           exp_w_gate, exp_b_gate, exp_w_up, exp_b_up, exp_w_down, exp_b_down, ln3_g, ln3_b):
    b, s, d = x.shape
    t = b * s
    x2d = x.reshape(t, d)
    for l in range(DEPTH):
        lambda_init = 0.8 - 0.6 * math.exp(-0.3 * l)
        n_g = 4 * GDN_W
        w = w_in[l]
        wg = w[:, :n_g].astype(BF16)
        small = w[:, n_g:n_g + 2 * GDN_HEADS]
        ws = jnp.pad(small, ((0, 0), (0, SMALL_W - 2 * GDN_HEADS))).astype(BF16)
        d0 = n_g + 2 * GDN_HEADS
        wdq = w[:, d0:d0 + DIFF_W].astype(BF16)
        wdk = w[:, d0 + DIFF_W:d0 + 2 * DIFF_W].astype(BF16)
        wdv = w[:, d0 + 2 * DIFF_W:d0 + 3 * DIFF_W].astype(BF16)
        conv_w = gdn_conv_w[l].reshape(GDN_CONV, 3 * GDN_W).astype(F32)
        ab = jnp.zeros((2, SMALL_W), F32)
        ab = ab.at[0, GDN_HEADS:2 * GDN_HEADS].set(-jnp.exp(gdn_a_log[l].astype(F32)))
        ab = ab.at[1, GDN_HEADS:2 * GDN_HEADS].set(gdn_dt_bias[l].astype(F32))
        lam = (jnp.exp(jnp.sum(diff_lq1[l].astype(F32) * diff_lk1[l].astype(F32)))
               - jnp.exp(jnp.sum(diff_lq2[l].astype(F32) * diff_lk2[l].astype(F32))) + lambda_init).reshape(1)

        g2d, dqt, dk2d, dvt, s2d = _inproj(x2d, wg, wdq, wdk, wdv, ws, tm=512, batch=b,
                                           q_scale=LOG2E * DIFF_DQK ** -0.5)
        o_gdn = _gdn(g2d.reshape(b, s, n_g), s2d.reshape(b, s, SMALL_W), conv_w, ab,
                     gdn_norm_w[l].reshape(1, GDN_DV).astype(F32), ts=256)
        o_diff = _diff_attn(lam, dqt, dk2d.reshape(b, s, DIFF_W), dvt,
                            diff_norm_w[l].reshape(DIFF_DV, 1).astype(F32), tq=256, out_scale=1.0 - lambda_init)

        dh = d // XA_HEADS
        kt, v = _memkv(mem, xa_wk[l].astype(BF16), xa_wv[l].astype(BF16), scale=dh ** -0.5)
        wo_mix = w_out[l].astype(BF16)
        rw = jnp.pad(router_w[l].astype(F32), ((0, 0), (0, LANES - N_EXPERTS)))
        rb = jnp.pad(router_b[l].astype(F32), (0, LANES - N_EXPERTS)).reshape(1, LANES)
        row = lambda a: a.reshape(1, -1).astype(F32)
        x2, x2p, route, gates, counts = _mid(
            x2d, o_gdn.reshape(t, GDN_W), o_diff.reshape(t, DIFF_HEADS * DIFF_DV),
            wo_mix[:GDN_W], wo_mix[GDN_W:], row(ln1_g[l]), row(ln1_b[l]),
            xa_wq[l].astype(BF16), kt, v, xa_wo[l].astype(BF16), row(ln2_g[l]), row(ln2_b[l]),
            rw, rb, tm=512, rows_per_batch=s)

        tm_moe = 256
        dest, block_e, n_used, n_blocks = _layout(route, counts, tm_moe)
        split = (d // 2) // SC_ROW_WORDS
        piece = lambda ix: (ix[..., None] * split + jnp.arange(split, dtype=I32)).reshape(ix.shape[:-1] + (-1,))
        p_rows = n_blocks * tm_moe
        xs = _sc_scatter_rows(x2p.reshape(t * split, SC_ROW_WORDS), piece(dest.T),
                              p_rows * split).reshape(p_rows, d // 2)
        ys = _moe(block_e, n_used, xs,
                  exp_w_gate[l], exp_b_gate[l].reshape(N_EXPERTS, 1, -1), exp_w_up[l],
                  exp_b_up[l].reshape(N_EXPERTS, 1, -1), exp_w_down[l], exp_b_down[l].reshape(N_EXPERTS, 1, -1),
                  tm=tm_moe)
        y4 = _sc_gather_rows(ys.reshape(p_rows * split, SC_ROW_WORDS),
                             piece(dest.reshape(-1))).reshape(t, TOP_K * (d // 2))
        x2d = _final(x2, y4, gates, row(ln3_g[l]), row(ln3_b[l]), tm=512)
    return x2d.reshape(b, s, d)
```

```python
import functools
import math

import jax
import jax.numpy as jnp
from jax import lax
from jax.experimental import pallas as pl
from jax.experimental.pallas import tpu as pltpu
from jax.experimental.pallas import tpu_sc as plsc

F32 = jnp.float32
BF16 = jnp.bfloat16
U32 = jnp.uint32
I32 = jnp.int32
HIGHEST = lax.Precision.HIGHEST

GDN_HEADS = 4
GDN_DK = 128
GDN_DV = 128
GDN_CONV = 4
GDN_CHUNK = 64
DIFF_HEADS = 4
DIFF_DQK = 64
DIFF_DV = 128
XA_HEADS = 4
N_EXPERTS = 32
TOP_K = 4
SWIGLU_ALPHA = 1.702
SWIGLU_LIMIT = 7.0
LN_EPS = 1e-5
RMS_EPS = 1e-6
DEPTH = 1
DEEPNORM_ALPHA = (2 * DEPTH) ** 0.25

GDN_W = GDN_HEADS * GDN_DK
DIFF_W = DIFF_HEADS * 2 * DIFF_DQK
SMALL_W = 128

LANES = 128
SUBLANES = 8
BF16_SUBLANES = 16
VMEM_LIMIT = 56 * 1024 * 1024
SC_WINDOW = 128
SC_ROW_WORDS = 256
MID_SUBBLOCKS = 2

NEG_BIG = -1e30
LOG2E = 1.4426950408889634


def _dot(a, b):
    return jnp.dot(a, b, preferred_element_type=F32)


def _dot_nt(a, b):
    return lax.dot_general(a, b, (((1,), (1,)), ((), ())), preferred_element_type=F32)


def _dot_tn(a, b):
    return lax.dot_general(a, b, (((0,), (0,)), ((), ())), preferred_element_type=F32)


def _layer_norm(y, g, b):
    mu = jnp.mean(y, axis=-1, keepdims=True)
    d = y - mu
    var = jnp.mean(d * d, axis=-1, keepdims=True)
    return d * lax.rsqrt(var + LN_EPS) * g + b


def _sigmoid(x):
    return 1.0 / (1.0 + jnp.exp(-x))


def _pack_bf16_pairs(x):
    n = x.shape[1] // 2
    bits = pltpu.bitcast(x.astype(BF16).astype(F32), U32)
    return (bits[:, :n] >> 16) | bits[:, n:]


def _unpack_bf16_pairs(w):
    lo = pltpu.bitcast(w << 16, F32)
    hi = pltpu.bitcast(w & jnp.uint32(0xFFFF0000), F32)
    return lo, hi


def _inproj_kernel(x_ref, wg_ref, wq_ref, wk_ref, wv_ref, ws_ref, g_ref, qt_ref, k_ref, vt_ref, s_ref, *, q_scale):
    xb = x_ref[...].astype(BF16)
    g_ref[...] = _dot(xb, wg_ref[...]).astype(BF16)
    qt_ref[0] = (_dot(xb, wq_ref[...]) * q_scale).T.astype(BF16)
    k_ref[...] = _dot(xb, wk_ref[...]).astype(BF16)
    vt_ref[0] = _dot(xb, wv_ref[...]).T.astype(BF16)
    s_ref[...] = _dot(xb, ws_ref[...])


def _inproj(x2d, wg, wq, wk, wv, ws, tm, batch, q_scale):
    t, d = x2d.shape
    s = t // batch
    per_batch = s // tm
    full = lambda a: pl.BlockSpec(a.shape, lambda i: (0, 0))
    row = lambda w: pl.BlockSpec((tm, w), lambda i: (i, 0))
    tr = lambda w: pl.BlockSpec((1, w, tm), lambda i: (i // per_batch, 0, i % per_batch))
    return pl.pallas_call(
        functools.partial(_inproj_kernel, q_scale=q_scale),
        grid=(t // tm,),
        in_specs=[row(d), full(wg), full(wq), full(wk), full(wv), full(ws)],
        out_specs=[row(wg.shape[1]), tr(wq.shape[1]), row(wk.shape[1]), tr(wv.shape[1]), row(ws.shape[1])],
        out_shape=[jax.ShapeDtypeStruct((t, wg.shape[1]), BF16),
                   jax.ShapeDtypeStruct((batch, wq.shape[1], s), BF16),
                   jax.ShapeDtypeStruct((t, wk.shape[1]), BF16),
                   jax.ShapeDtypeStruct((batch, wv.shape[1], s), BF16),
                   jax.ShapeDtypeStruct((t, ws.shape[1]), F32)],
        compiler_params=pltpu.CompilerParams(dimension_semantics=("parallel",),
                                             vmem_limit_bytes=VMEM_LIMIT),
        name="inproj",
    )(x2d, wg, wq, wk, wv, ws)


def _gdn_kernel(q_ref, k_ref, v_ref, z_ref, hq_ref, hk_ref, hv_ref, s_ref, cw_ref, ab_ref, nw_ref,
                o_ref, state_ref, *, ts):
    c_len = GDN_CHUNK
    n_chunks = ts // c_len
    nh = GDN_HEADS
    i = pl.program_id(1)

    @pl.when(i == 0)
    def _():
        state_ref[...] = jnp.zeros_like(state_ref)

    row8 = lax.broadcasted_iota(I32, (SUBLANES, GDN_W), 0)
    first = i == 0

    def conv_silu(x_ref_, h_ref_, col0):
        x = x_ref_[0].astype(F32)
        hb = jnp.where(first, 0.0, h_ref_[0].astype(F32))
        w = cw_ref[:, col0:col0 + GDN_W]
        y = x * w[GDN_CONV - 1:GDN_CONV, :]
        for j in range(1, GDN_CONV):
            xr = pltpu.roll(x, j, 0)
            fix = pltpu.roll(hb, j, 0)
            top = jnp.where(row8 < j, fix, xr[0:SUBLANES])
            xr = jnp.concatenate([top, xr[SUBLANES:]], axis=0)
            y = y + xr * w[GDN_CONV - 1 - j:GDN_CONV - j, :]
        return y * _sigmoid(y)

    qa = conv_silu(q_ref, hq_ref, 0)
    ka = conv_silu(k_ref, hk_ref, GDN_W)
    va = conv_silu(v_ref, hv_ref, 2 * GDN_W)

    sg = s_ref[0]
    beta_all = _sigmoid(sg)
    sp_in = sg + ab_ref[1:2, :]
    softplus = jnp.maximum(sp_in, 0.0) + jnp.log(1.0 + jnp.exp(-jnp.abs(sp_in)))
    g_step = ab_ref[0:1, :] * softplus

    r = lax.broadcasted_iota(I32, (ts, ts), 0)
    c = lax.broadcasted_iota(I32, (ts, ts), 1)
    tri = jnp.where((r // c_len) == (c // c_len), jnp.where(c <= r, 1.0, 0.0), 0.0)
    gc = jnp.dot(tri, g_step, precision=HIGHEST, preferred_element_type=F32)
    gct = [gc[cc * c_len:(cc + 1) * c_len, :].T for cc in range(n_chunks)]

    ri = lax.broadcasted_iota(I32, (c_len, c_len), 0)
    ci = lax.broadcasted_iota(I32, (c_len, c_len), 1)
    causal = ci <= ri
    strict = ci < ri
    nw = nw_ref[...]

    pairs = [(h, cc) for h in range(nh) for cc in range(n_chunks)]
    qs, ks, kbs, vbs, kbes, decays, qds, kts, gls = [], [], [], [], [], [], [], [], []
    for h in range(nh):
        lo, hi = h * GDN_DK, (h + 1) * GDN_DK
        qh = qa[:, lo:hi]
        kh = ka[:, lo:hi]
        vh = va[:, lo:hi]
        qh = qh * (lax.rsqrt(jnp.sum(qh * qh, axis=-1, keepdims=True) + RMS_EPS) * (GDN_DK ** -0.5))
        kh = kh * lax.rsqrt(jnp.sum(kh * kh, axis=-1, keepdims=True) + RMS_EPS)
        beta = beta_all[:, h:h + 1]
        gcol_all = gc[:, nh + h:nh + h + 1]
        for cc in range(n_chunks):
            r0, r1 = cc * c_len, (cc + 1) * c_len
            qc, kc, vc = qh[r0:r1], kh[r0:r1], vh[r0:r1]
            bcol = beta[r0:r1]
            gcol = gcol_all[r0:r1]
            grow = gct[cc][nh + h:nh + h + 1, :]
            decay = jnp.where(causal, jnp.exp(jnp.where(causal, gcol - grow, 0.0)), 0.0)
            eg = jnp.exp(gcol)
            g_last = gcol[c_len - 1:c_len, :]
            kb = kc * bcol
            qs.append(qc.astype(BF16))
            ks.append(kc.astype(BF16))
            kbs.append(kb)
            vbs.append(vc * bcol)
            kbes.append(kb * eg)
            decays.append(decay)
            qds.append((qc * eg).astype(BF16))
            kts.append((kc * jnp.exp(g_last - gcol)).astype(BF16))
            gls.append(jnp.exp(g_last))

    n = len(pairs)
    ms = [-jnp.where(strict, _dot_nt(kbs[b].astype(BF16), ks[b]) * decays[b], 0.0) for b in range(n)]
    ais = [jnp.where(causal, _dot_nt(qs[b], ks[b]) * decays[b], 0.0).astype(BF16) for b in range(n)]
    ys = ms
    for _ in range(5):
        mb = [m.astype(BF16) for m in ms]
        ms = [_dot(mb[b], mb[b]) for b in range(n)]
        mb = [m.astype(BF16) for m in ms]
        ys = [ys[b] + ms[b] + _dot(ys[b].astype(BF16), mb[b]) for b in range(n)]
    yb = [y.astype(BF16) for y in ys]
    us = [vbs[b] + _dot(yb[b], vbs[b].astype(BF16)) for b in range(n)]
    ws = [(kbes[b] + _dot(yb[b], kbes[b].astype(BF16))).astype(BF16) for b in range(n)]

    states = [state_ref[h] for h in range(nh)]
    outs = [[None] * n_chunks for _ in range(nh)]
    for cc in range(n_chunks):
        idx = [h * n_chunks + cc for h in range(nh)]
        sb = [st.astype(BF16) for st in states]
        v_new = [us[idx[h]] - _dot(ws[idx[h]], sb[h]) for h in range(nh)]
        vnb = [v.astype(BF16) for v in v_new]
        for h in range(nh):
            outs[h][cc] = _dot(qds[idx[h]], sb[h]) + _dot(ais[idx[h]], vnb[h])
        states = [states[h] * gls[idx[h]] + _dot_tn(kts[idx[h]], vnb[h]) for h in range(nh)]
    for h in range(nh):
        lo, hi = h * GDN_DK, (h + 1) * GDN_DK
        state_ref[h] = states[h]
        o = jnp.concatenate(outs[h], axis=0)
        o = o * lax.rsqrt(jnp.mean(o * o, axis=-1, keepdims=True) + RMS_EPS) * nw
        z = z_ref[0, :, lo:hi].astype(F32)
        o_ref[0, :, lo:hi] = (o * (z * _sigmoid(z))).astype(o_ref.dtype)


def _gdn(g3, s3, conv_w, ab, nw, ts):
    b, s, _ = g3.shape
    per_block = ts // SUBLANES

    def blk(col):
        return pl.BlockSpec((1, ts, GDN_W), lambda bb, i: (bb, i, col))

    def halo(col):
        return pl.BlockSpec((1, SUBLANES, GDN_W),
                            lambda bb, i: (bb, jnp.maximum(i * per_block - 1, 0), col))

    return pl.pallas_call(
        functools.partial(_gdn_kernel, ts=ts),
        grid=(b, s // ts),
        in_specs=[blk(0), blk(1), blk(2), blk(3), halo(0), halo(1), halo(2),
                  pl.BlockSpec((1, ts, SMALL_W), lambda bb, i: (bb, i, 0)),
                  pl.BlockSpec(conv_w.shape, lambda bb, i: (0, 0)),
                  pl.BlockSpec(ab.shape, lambda bb, i: (0, 0)),
                  pl.BlockSpec(nw.shape, lambda bb, i: (0, 0))],
        out_specs=pl.BlockSpec((1, ts, GDN_W), lambda bb, i: (bb, i, 0)),
        out_shape=jax.ShapeDtypeStruct((b, s, GDN_W), BF16),
        scratch_shapes=[pltpu.VMEM((GDN_HEADS, GDN_DK, GDN_DV), F32)],
        compiler_params=pltpu.CompilerParams(dimension_semantics=("parallel", "arbitrary"),
                                             vmem_limit_bytes=VMEM_LIMIT),
        name="gdn",
    )(g3, g3, g3, g3, g3, g3, g3, s3, conv_w, ab, nw)


def _diff_kernel(lam_ref, qt_ref, k_ref, vt_ref, nw_ref, o_ref, m_ref, acc_ref, s_ref, cm_ref, *, tq, tk,
                 out_scale):
    i = pl.program_id(2)
    qt = qt_ref[0]
    rowi = lax.broadcasted_iota(I32, (2 * DIFF_DQK, tq), 0)
    zero = jnp.zeros_like(qt)
    qq = jnp.concatenate([jnp.where(rowi < DIFF_DQK, qt, zero),
                          jnp.where(rowi >= DIFF_DQK, qt, zero)], axis=1)
    m_ref[...] = jnp.full(m_ref.shape, NEG_BIG, F32)
    acc_ref[...] = jnp.zeros(acc_ref.shape, F32)
    ones_rows = jnp.ones((BF16_SUBLANES, tk), BF16)

    def scores(key_block, slot, diag):
        start = pl.multiple_of(key_block * tk, tk)
        s = _dot(k_ref[0, pl.ds(start, tk), :], qq)
        if diag is not None:
            kr = lax.broadcasted_iota(I32, (tk, 2 * tq), 0) + diag * tk
            qc = lax.broadcasted_iota(I32, (tk, 2 * tq), 1)
            qc = jnp.where(qc >= tq, qc - tq, qc)
            s = jnp.where(kr <= qc, s, NEG_BIG)
        s_ref[slot] = s
        cm_ref[slot] = jnp.max(s, axis=0, keepdims=True)

    def update(key_block, slot):
        start = pl.multiple_of(key_block * tk, tk)
        vtj = vt_ref[0, :, pl.ds(start, tk)]
        m_old = m_ref[...]
        m_new = jnp.maximum(m_old, cm_ref[slot])
        alpha = jnp.exp2(m_old - m_new)
        p = jnp.exp2(s_ref[slot] - m_new).astype(BF16)
        vt_ext = jnp.concatenate([vtj, ones_rows], axis=0)
        acc_ref[...] = alpha * acc_ref[...] + _dot(vt_ext, p)
        m_ref[...] = m_new

    base = 2 * i
    scores(base, 0, 0)
    scores(base + 1, 1, 1)
    update(base, 0)

    def pair(jj, carry):
        prev = jnp.where(jj == 0, base + 1, 2 * jj - 1)
        scores(2 * jj, 0, None)
        update(prev, 1)
        scores(2 * jj + 1, 1, None)
        update(2 * jj, 0)
        return carry

    lax.fori_loop(0, i, pair, 0)
    update(jnp.where(i == 0, base + 1, base - 1), 1)

    acc = acc_ref[...]
    o = acc[:DIFF_DV] / acc[DIFF_DV:DIFF_DV + 1]
    od = o[:, :tq] - lam_ref[0] * o[:, tq:]
    od = od * lax.rsqrt(jnp.mean(od * od, axis=0, keepdims=True) + RMS_EPS) * nw_ref[...] * out_scale
    o_ref[0] = od.T.astype(o_ref.dtype)


def _diff_attn(lam, qt, k3, vt, nw_col, tq, out_scale):
    b, s, _ = k3.shape
    nh = DIFF_HEADS
    tk = tq // 2
    return pl.pallas_call(
        functools.partial(_diff_kernel, tq=tq, tk=tk, out_scale=out_scale),
        grid=(b, nh, s // tq),
        in_specs=[pl.BlockSpec(memory_space=pltpu.SMEM),
                  pl.BlockSpec((1, LANES, tq), lambda bb, h, i: (bb, h, i)),
                  pl.BlockSpec((1, s, LANES), lambda bb, h, i: (bb, 0, h)),
                  pl.BlockSpec((1, LANES, s), lambda bb, h, i: (bb, h, 0)),
                  pl.BlockSpec(nw_col.shape, lambda bb, h, i: (0, 0))],
        out_specs=pl.BlockSpec((1, tq, LANES), lambda bb, h, i: (bb, i, h)),
        out_shape=jax.ShapeDtypeStruct((b, s, nh * DIFF_DV), BF16),
        scratch_shapes=[pltpu.VMEM((1, 2 * tq), F32), pltpu.VMEM((DIFF_DV + BF16_SUBLANES, 2 * tq), F32),
                        pltpu.VMEM((2, tk, 2 * tq), F32), pltpu.VMEM((2, 1, 2 * tq), F32)],
        compiler_params=pltpu.CompilerParams(dimension_semantics=("parallel", "parallel", "arbitrary"),
                                             vmem_limit_bytes=VMEM_LIMIT),
        name="diff_attn",
    )(lam, qt, k3, vt, nw_col)


def _memkv_kernel(mem_ref, wk_ref, wv_ref, kt_ref, v_ref, *, scale):
    mb = mem_ref[0].astype(BF16)
    k = _dot(mb, wk_ref[...])
    kt_ref[0] = (k.T * scale).astype(BF16)
    v_ref[0] = _dot(mb, wv_ref[...]).astype(BF16)


def _memkv(mem, wk, wv, scale):
    b, m, d = mem.shape
    return pl.pallas_call(
        functools.partial(_memkv_kernel, scale=scale),
        grid=(b,),
        in_specs=[pl.BlockSpec((1, m, d), lambda bb: (bb, 0, 0)),
                  pl.BlockSpec(wk.shape, lambda bb: (0, 0)),
                  pl.BlockSpec(wv.shape, lambda bb: (0, 0))],
        out_specs=[pl.BlockSpec((1, d, m), lambda bb: (bb, 0, 0)),
                   pl.BlockSpec((1, m, d), lambda bb: (bb, 0, 0))],
        out_shape=[jax.ShapeDtypeStruct((b, d, m), BF16), jax.ShapeDtypeStruct((b, m, d), BF16)],
        compiler_params=pltpu.CompilerParams(dimension_semantics=("parallel",),
                                             vmem_limit_bytes=VMEM_LIMIT),
        name="memkv",
    )(mem, wk, wv)


def _mid_kernel(x_ref, og_ref, od_ref, wo1_ref, wo2_ref, g1_ref, b1_ref, wq_ref, kt_ref, v_ref, wo_ref,
                g2_ref, b2_ref, rw_ref, rw1_ref, rb_ref, x2_ref, x2p_ref, route_ref, gate_ref, cnt_ref, run_ref):
    i = pl.program_id(0)

    @pl.when(i == 0)
    def _():
        run_ref[...] = jnp.zeros_like(run_ref)

    tm, d = x_ref.shape
    sub = tm // MID_SUBBLOCKS
    spans = [(k * sub, (k + 1) * sub) for k in range(MID_SUBBLOCKS)]
    dh = d // XA_HEADS

    h = [_dot(og_ref[a:b, :], wo1_ref[...]) + _dot(od_ref[a:b, :], wo2_ref[...]) for a, b in spans]
    x1 = [_layer_norm(DEEPNORM_ALPHA * x_ref[a:b, :] + hk, g1_ref[...], b1_ref[...])
          for (a, b), hk in zip(spans, h)]
    q = [_dot(xk.astype(BF16), wq_ref[...]).astype(BF16) for xk in x1]
    heads = [[] for _ in spans]
    for hh in range(XA_HEADS):
        lo, hi = hh * dh, (hh + 1) * dh
        s = [_dot(qk[:, lo:hi], kt_ref[0, lo:hi, :]) for qk in q]
        p = [jnp.exp(sk - jnp.max(sk, axis=-1, keepdims=True)) for sk in s]
        p = [(pk / jnp.sum(pk, axis=-1, keepdims=True)).astype(BF16) for pk in p]
        for k, pk in enumerate(p):
            heads[k].append(_dot(pk, v_ref[0, :, lo:hi]))
    o = [jnp.concatenate(hk, axis=-1).astype(BF16) for hk in heads]
    h2 = [_dot(ok, wo_ref[...]) for ok in o]
    x2 = [_layer_norm(DEEPNORM_ALPHA * xk + hk, g2_ref[...], b2_ref[...]) for xk, hk in zip(x1, h2)]

    x_hi = [xk.astype(BF16) for xk in x2]
    x_lo = [(xk - hk.astype(F32)).astype(BF16) for xk, hk in zip(x2, x_hi)]
    r1 = [_dot(hk, rw_ref[...]) for hk in x_hi]
    r2 = [_dot(lk, rw1_ref[...]) for lk in x_lo]
    logits = [a1 + pltpu.roll(a1, LANES - N_EXPERTS, 1) + a2 + rb_ref[...] for a1, a2 in zip(r1, r2)]

    lane = lax.broadcasted_iota(I32, (sub, LANES), 1)
    rr = lax.broadcasted_iota(I32, (sub, sub), 0)
    cc = lax.broadcasted_iota(I32, (sub, sub), 1)
    tri = jnp.where(cc < rr, 1.0, 0.0).astype(BF16)
    run = run_ref[...]
    for (a, b), xk, lgk in zip(spans, x2, logits):
        x2_ref[a:b, :] = xk
        packed = _pack_bf16_pairs(xk)
        for piece in range(x2p_ref.shape[0]):
            x2p_ref[piece, a:b, :] = packed[:, piece * SC_ROW_WORDS:(piece + 1) * SC_ROW_WORDS]
        lg = jnp.where(lane < N_EXPERTS, lgk, NEG_BIG)
        sel = jnp.zeros((sub, LANES), F32)
        vals, hots, idxs = [], [], []
        for _ in range(TOP_K):
            mx = jnp.max(lg, axis=-1, keepdims=True)
            idx = jnp.min(jnp.where(lg == mx, lane, LANES), axis=-1, keepdims=True)
            hot = lane == idx
            vals.append(mx)
            idxs.append(idx)
            hots.append(hot)
            lg = jnp.where(hot, NEG_BIG, lg)
            sel = sel + jnp.where(hot, 1.0, 0.0)
        ex = [jnp.exp(v - vals[0]) for v in vals]
        den = ex[0] + ex[1] + ex[2] + ex[3]
        before = _dot(tri, sel.astype(BF16)) + run
        route = jnp.zeros((sub, LANES), F32)
        gates = jnp.zeros((sub, LANES), F32)
        for kk in range(TOP_K):
            rank = jnp.sum(jnp.where(hots[kk], before, 0.0), axis=-1, keepdims=True)
            route = jnp.where(lane == kk, idxs[kk].astype(F32), route)
            route = jnp.where(lane == TOP_K + kk, rank, route)
            gates = jnp.where(lane == kk, ex[kk] / den, gates)
        route_ref[:, a:b] = route.T[:2 * TOP_K].astype(I32)
        gate_ref[a:b, :] = gates
        run = run + jnp.sum(sel, axis=0, keepdims=True)
    run_ref[...] = run
    cnt_ref[...] = run.astype(I32)


def _mid(x2d, og, od, wo1, wo2, g1, b1, wq, kt, v, wo, g2, b2, rw, rw1, rb, tm, rows_per_batch):
    t, d = x2d.shape
    blocks_per_batch = rows_per_batch // tm
    pieces = (d // 2) // SC_ROW_WORDS
    row = lambda w: pl.BlockSpec((tm, w), lambda i: (i, 0))
    full = lambda a: pl.BlockSpec(a.shape, lambda i: (0, 0))
    per_batch = lambda a: pl.BlockSpec((1,) + a.shape[1:], lambda i: (i // blocks_per_batch, 0, 0))
    return pl.pallas_call(
        _mid_kernel,
        grid=(t // tm,),
        in_specs=[row(d), row(og.shape[1]), row(od.shape[1]), full(wo1), full(wo2), full(g1), full(b1),
                  full(wq), per_batch(kt), per_batch(v), full(wo), full(g2), full(b2), full(rw), full(rw1), full(rb)],
        out_specs=[row(d), pl.BlockSpec((pieces, tm, SC_ROW_WORDS), lambda i: (0, i, 0)),
                   pl.BlockSpec((2 * TOP_K, tm), lambda i: (0, i)), row(LANES),
                   pl.BlockSpec((1, LANES), lambda i: (0, 0))],
        out_shape=[jax.ShapeDtypeStruct((t, d), F32), jax.ShapeDtypeStruct((pieces, t, SC_ROW_WORDS), U32),
                   jax.ShapeDtypeStruct((2 * TOP_K, t), I32), jax.ShapeDtypeStruct((t, LANES), F32),
                   jax.ShapeDtypeStruct((1, LANES), I32)],
        scratch_shapes=[pltpu.VMEM((1, LANES), F32)],
        compiler_params=pltpu.CompilerParams(dimension_semantics=("arbitrary",),
                                             vmem_limit_bytes=VMEM_LIMIT),
        name="mid",
    )(x2d, og, od, wo1, wo2, g1, b1, wq, kt, v, wo, g2, b2, rw, rw1, rb)


def _sc_scatter_rows(rows, idx4, n_out):
    t, d = rows.shape
    mesh = plsc.VectorSubcoreMesh(core_axis_name="c", subcore_axis_name="s")

    @functools.partial(pl.kernel, out_type=jax.ShapeDtypeStruct((n_out, d), rows.dtype), mesh=mesh)
    def scatter_kernel(x_hbm, i0_hbm, i1_hbm, i2_hbm, i3_hbm, o_hbm):
        def body(x_vmem, i0, i1, i2, i3):
            for iv in (i0, i1, i2, i3):
                pltpu.sync_copy(x_vmem, o_hbm.at[iv.at[0]])

        ispec = pl.BlockSpec((1, SC_WINDOW), lambda i: (0, i))
        pltpu.emit_pipeline(
            body,
            grid=(t // SC_WINDOW,),
            in_specs=[pl.BlockSpec((SC_WINDOW, d), lambda i: (i, 0)), ispec, ispec, ispec, ispec],
            out_specs=[],
            core_axis_name=("c", "s"),
            dimension_semantics=(pltpu.PARALLEL,),
        )(x_hbm, i0_hbm, i1_hbm, i2_hbm, i3_hbm)

    return scatter_kernel(rows, *[idx4[k:k + 1] for k in range(TOP_K)])


def _sc_gather_rows(table, idx):
    n = idx.shape[0]
    d = table.shape[1]
    mesh = plsc.VectorSubcoreMesh(core_axis_name="c", subcore_axis_name="s")

    @functools.partial(pl.kernel, out_type=jax.ShapeDtypeStruct((n, d), table.dtype), mesh=mesh)
    def gather_kernel(x_hbm, i_hbm, o_hbm):
        def body(i_vmem, o_vmem):
            pltpu.sync_copy(x_hbm.at[i_vmem.at[0]], o_vmem)

        pltpu.emit_pipeline(
            body,
            grid=(n // SC_WINDOW,),
            in_specs=[pl.BlockSpec((1, SC_WINDOW), lambda i: (0, i))],
            out_specs=[pl.BlockSpec((SC_WINDOW, d), lambda i: (i, 0))],
            core_axis_name=("c", "s"),
            dimension_semantics=(pltpu.PARALLEL,),
        )(i_hbm, o_hbm)

    return gather_kernel(table, idx.reshape(1, n))


def _moe_kernel(be_ref, nu_ref, x_ref, wg_ref, bg_ref, wu_ref, bu_ref, wd_ref, bd_ref, y_ref,
                wgb_ref, wub_ref, wdb_ref):
    i = pl.program_id(0)
    prev = be_ref[jnp.maximum(i - 1, 0)]
    changed = jnp.logical_or(i == 0, be_ref[i] != prev)

    @pl.when(changed)
    def _():
        wgb_ref[...] = wg_ref[0].astype(BF16)
        wub_ref[...] = wu_ref[0].astype(BF16)
        wdb_ref[...] = wd_ref[0].astype(BF16)

    @pl.when(i < nu_ref[0])
    def _():
        halves = [_unpack_bf16_pairs(x_ref[piece]) for piece in range(x_ref.shape[0])]
        x = jnp.concatenate([h[0] for h in halves] + [h[1] for h in halves], axis=-1).astype(BF16)
        hg = _dot(x, wgb_ref[...]) + bg_ref[0]
        hl = _dot(x, wub_ref[...]) + bu_ref[0]
        hg = jnp.minimum(hg, SWIGLU_LIMIT)
        hl = jnp.clip(hl, -SWIGLU_LIMIT, SWIGLU_LIMIT)
        act = hg * _sigmoid(SWIGLU_ALPHA * hg) * (hl + 1.0)
        y = _dot(act.astype(BF16), wdb_ref[...]) + bd_ref[0]
        packed = _pack_bf16_pairs(y)
        for piece in range(y_ref.shape[0]):
            y_ref[piece] = packed[:, piece * SC_ROW_WORDS:(piece + 1) * SC_ROW_WORDS]

    @pl.when(i >= nu_ref[0])
    def _():
        y_ref[...] = jnp.zeros_like(y_ref)


def _moe(block_e, n_used, xs, wg, bg, wu, bu, wd, bd, tm):
    pieces, p, dp = xs.shape
    d, dff = wg.shape[1], wg.shape[2]
    n_blocks = p // tm
    wspec = lambda a: pl.BlockSpec((1,) + a.shape[1:], lambda i, be, nu: (be[i], 0, 0))
    rows = pl.BlockSpec((pieces, tm, dp), lambda i, be, nu: (0, i, 0))
    grid_spec = pltpu.PrefetchScalarGridSpec(
        num_scalar_prefetch=2,
        grid=(n_blocks,),
        in_specs=[rows, wspec(wg), wspec(bg), wspec(wu), wspec(bu), wspec(wd), wspec(bd)],
        out_specs=rows,
        scratch_shapes=[pltpu.VMEM((d, dff), BF16), pltpu.VMEM((d, dff), BF16), pltpu.VMEM((dff, d), BF16)],
    )
    return pl.pallas_call(
        _moe_kernel,
        grid_spec=grid_spec,
        out_shape=jax.ShapeDtypeStruct((pieces, p, dp), U32),
        compiler_params=pltpu.CompilerParams(dimension_semantics=("arbitrary",),
                                             vmem_limit_bytes=VMEM_LIMIT),
        name="moe_ffn",
    )(block_e, n_used, xs, wg, bg, wu, bu, wd, bd)


def _final_kernel(x_ref, y4_ref, gate_ref, g_ref, b_ref, o_ref):
    pieces = y4_ref.shape[0]
    gates = gate_ref[...]
    lows = [0.0] * pieces
    highs = [0.0] * pieces
    for kk in range(TOP_K):
        gk = gates[:, kk:kk + 1]
        for piece in range(pieces):
            lo, hi = _unpack_bf16_pairs(y4_ref[piece, kk])
            lows[piece] = lows[piece] + gk * lo
            highs[piece] = highs[piece] + gk * hi
    y = jnp.concatenate(lows + highs, axis=-1)
    o_ref[...] = _layer_norm(DEEPNORM_ALPHA * x_ref[...] + y, g_ref[...], b_ref[...])


def _final(x2, y4, gates, g, b, tm):
    t, d = x2.shape
    row = lambda w: pl.BlockSpec((tm, w), lambda i: (i, 0))
    full = lambda a: pl.BlockSpec(a.shape, lambda i: (0, 0))
    return pl.pallas_call(
        _final_kernel,
        grid=(t // tm,),
        in_specs=[row(d), pl.BlockSpec(y4.shape[:2] + (tm, y4.shape[3]), lambda i: (0, 0, i, 0)),
                  row(gates.shape[1]), full(g), full(b)],
        out_specs=row(d),
        out_shape=jax.ShapeDtypeStruct((t, d), F32),
        compiler_params=pltpu.CompilerParams(dimension_semantics=("parallel",),
                                             vmem_limit_bytes=VMEM_LIMIT),
        name="final_ln",
    )(x2, y4, gates, g, b)


def _layout(route, counts, tm):
    t = route.shape[1]
    experts = route[:TOP_K]
    ranks = route[TOP_K:]
    counts = counts[0, :N_EXPERTS]
    padded = (counts + tm - 1) // tm * tm
    pad_end = jnp.cumsum(padded)
    pad_start = pad_end - padded
    eids = jnp.arange(N_EXPERTS, dtype=I32)[:, None, None]
    dest = ranks + jnp.sum(jnp.where(experts[None] == eids, pad_start[:, None, None], 0), axis=0)
    n_blocks = t * TOP_K // tm + N_EXPERTS
    starts = jnp.arange(n_blocks, dtype=I32) * tm
    block_e = jnp.minimum(jnp.sum((pad_end[None, :] <= starts[:, None]).astype(I32), axis=-1), N_EXPERTS - 1)
    n_used = (pad_end[-1] // tm).astype(I32).reshape(1)
    return dest.astype(I32), block_e.astype(I32), n_used, n_blocks


def kernel(x, mem, w_in, gdn_conv_w, gdn_a_log, gdn_dt_bias, gdn_norm_w, diff_lq1, diff_lk1, diff_lq2, diff_lk2,
           diff_norm_w, w_out, ln1_g, ln1_b, xa_wq, xa_wk, xa_wv, xa_wo, ln2_g, ln2_b, router_w, router_b,
           exp_w_gate, exp_b_gate, exp_w_up, exp_b_up, exp_w_down, exp_b_down, ln3_g, ln3_b):
    b, s, d = x.shape
    t = b * s
    x2d = x.reshape(t, d)
    for l in range(DEPTH):
        lambda_init = 0.8 - 0.6 * math.exp(-0.3 * l)
        n_g = 4 * GDN_W
        w = w_in[l]
        wg = w[:, :n_g].astype(BF16)
        small = w[:, n_g:n_g + 2 * GDN_HEADS]
        ws = jnp.pad(small, ((0, 0), (0, SMALL_W - 2 * GDN_HEADS))).astype(BF16)
        d0 = n_g + 2 * GDN_HEADS
        wdq = w[:, d0:d0 + DIFF_W].astype(BF16)
        wdk = w[:, d0 + DIFF_W:d0 + 2 * DIFF_W].astype(BF16)
        wdv = w[:, d0 + 2 * DIFF_W:d0 + 3 * DIFF_W].astype(BF16)
        conv_w = gdn_conv_w[l].reshape(GDN_CONV, 3 * GDN_W).astype(F32)
        ab = jnp.zeros((2, SMALL_W), F32)
        ab = ab.at[0, GDN_HEADS:2 * GDN_HEADS].set(-jnp.exp(gdn_a_log[l].astype(F32)))
        ab = ab.at[1, GDN_HEADS:2 * GDN_HEADS].set(gdn_dt_bias[l].astype(F32))
        lam = (jnp.exp(jnp.sum(diff_lq1[l].astype(F32) * diff_lk1[l].astype(F32)))
               - jnp.exp(jnp.sum(diff_lq2[l].astype(F32) * diff_lk2[l].astype(F32))) + lambda_init).reshape(1)

        g2d, dqt, dk2d, dvt, s2d = _inproj(x2d, wg, wdq, wdk, wdv, ws, tm=512, batch=b,
                                           q_scale=LOG2E * DIFF_DQK ** -0.5)
        o_gdn = _gdn(g2d.reshape(b, s, n_g), s2d.reshape(b, s, SMALL_W), conv_w, ab,
                     gdn_norm_w[l].reshape(1, GDN_DV).astype(F32), ts=256)
        o_diff = _diff_attn(lam, dqt, dk2d.reshape(b, s, DIFF_W), dvt,
                            diff_norm_w[l].reshape(DIFF_DV, 1).astype(F32), tq=512, out_scale=1.0 - lambda_init)

        dh = d // XA_HEADS
        kt, v = _memkv(mem, xa_wk[l].astype(BF16), xa_wv[l].astype(BF16), scale=dh ** -0.5)
        wo_mix = w_out[l].astype(BF16)
        rw_f = router_w[l].astype(F32)
        rw_hi = rw_f.astype(BF16)
        rw_lo = (rw_f - rw_hi.astype(F32)).astype(BF16)
        rw = jnp.pad(jnp.concatenate([rw_hi, rw_lo], axis=1), ((0, 0), (0, LANES - 2 * N_EXPERTS)))
        rw1 = jnp.pad(rw_hi, ((0, 0), (0, LANES - N_EXPERTS)))
        rb = jnp.pad(router_b[l].astype(F32), (0, LANES - N_EXPERTS)).reshape(1, LANES)
        row = lambda a: a.reshape(1, -1).astype(F32)
        x2, x2p, route, gates, counts = _mid(
            x2d, o_gdn.reshape(t, GDN_W), o_diff.reshape(t, DIFF_HEADS * DIFF_DV),
            wo_mix[:GDN_W], wo_mix[GDN_W:], row(ln1_g[l]), row(ln1_b[l]),
            xa_wq[l].astype(BF16), kt, v, xa_wo[l].astype(BF16), row(ln2_g[l]), row(ln2_b[l]),
            rw, rw1, rb, tm=512, rows_per_batch=s)

        tm_moe = 256
        dest, block_e, n_used, n_blocks = _layout(route, counts, tm_moe)
        pieces = x2p.shape[0]
        p_rows = n_blocks * tm_moe
        offs = (jnp.arange(pieces, dtype=I32) * p_rows)[:, None]
        idx_scatter = (dest[:, None, :] + offs[None]).reshape(TOP_K, pieces * t)
        xs = _sc_scatter_rows(x2p.reshape(pieces * t, SC_ROW_WORDS), idx_scatter,
                              pieces * p_rows).reshape(pieces, p_rows, SC_ROW_WORDS)
        ys = _moe(block_e, n_used, xs,
                  exp_w_gate[l], exp_b_gate[l].reshape(N_EXPERTS, 1, -1), exp_w_up[l],
                  exp_b_up[l].reshape(N_EXPERTS, 1, -1), exp_w_down[l], exp_b_down[l].reshape(N_EXPERTS, 1, -1),
                  tm=tm_moe)
        idx_gather = (dest.reshape(1, TOP_K * t) + offs).reshape(pieces * TOP_K * t)
        y4 = _sc_gather_rows(ys.reshape(pieces * p_rows, SC_ROW_WORDS),
                             idx_gather).reshape(pieces, TOP_K, t, SC_ROW_WORDS)
        x2d = _final(x2, y4, gates, row(ln3_g[l]), row(ln3_b[l]), tm=512)
    return x2d.reshape(b, s, d)
```

```python
import functools
import math

import jax
import jax.numpy as jnp
from jax import lax
from jax.experimental import pallas as pl
from jax.experimental.pallas import tpu as pltpu
from jax.experimental.pallas import tpu_sc as plsc

F32 = jnp.float32
BF16 = jnp.bfloat16
U32 = jnp.uint32
I32 = jnp.int32
HIGHEST = lax.Precision.HIGHEST

GDN_HEADS = 4
GDN_DK = 128
GDN_DV = 128
GDN_CONV = 4
GDN_CHUNK = 64
DIFF_HEADS = 4
DIFF_DQK = 64
DIFF_DV = 128
XA_HEADS = 4
N_EXPERTS = 32
TOP_K = 4
SWIGLU_ALPHA = 1.702
SWIGLU_LIMIT = 7.0
LN_EPS = 1e-5
RMS_EPS = 1e-6
DEPTH = 1
DEEPNORM_ALPHA = (2 * DEPTH) ** 0.25

GDN_W = GDN_HEADS * GDN_DK
DIFF_W = DIFF_HEADS * 2 * DIFF_DQK
SMALL_W = 128

LANES = 128
SUBLANES = 8
BF16_SUBLANES = 16
VMEM_LIMIT = 56 * 1024 * 1024
SC_WINDOW = 128
SC_ROW_WORDS = 256
MID_SUBBLOCKS = 2

NEG_BIG = -1e30
LOG2E = 1.4426950408889634


def _dot(a, b):
    return jnp.dot(a, b, preferred_element_type=F32)


def _dot_nt(a, b):
    return lax.dot_general(a, b, (((1,), (1,)), ((), ())), preferred_element_type=F32)


def _dot_tn(a, b):
    return lax.dot_general(a, b, (((0,), (0,)), ((), ())), preferred_element_type=F32)


def _layer_norm(y, g, b):
    mu = jnp.mean(y, axis=-1, keepdims=True)
    d = y - mu
    var = jnp.mean(d * d, axis=-1, keepdims=True)
    return d * lax.rsqrt(var + LN_EPS) * g + b


def _sigmoid(x):
    return 1.0 / (1.0 + jnp.exp(-x))


def _pack_bf16_pairs(x):
    n = x.shape[1] // 2
    bits = pltpu.bitcast(x.astype(BF16).astype(F32), U32)
    return (bits[:, :n] >> 16) | bits[:, n:]


def _unpack_bf16_pairs(w):
    lo = pltpu.bitcast(w << 16, F32)
    hi = pltpu.bitcast(w & jnp.uint32(0xFFFF0000), F32)
    return lo, hi


def _inproj_kernel(x_ref, wg_ref, wq_ref, wk_ref, wv_ref, ws_ref, g_ref, qt_ref, k_ref, vt_ref, s_ref, *, q_scale):
    xb = x_ref[...].astype(BF16)
    g_ref[...] = _dot(xb, wg_ref[...]).astype(BF16)
    qt_ref[0] = (_dot(xb, wq_ref[...]) * q_scale).T.astype(BF16)
    k_ref[...] = _dot(xb, wk_ref[...]).astype(BF16)
    vt_ref[0] = _dot(xb, wv_ref[...]).T.astype(BF16)
    s_ref[...] = _dot(xb, ws_ref[...])


def _inproj(x2d, wg, wq, wk, wv, ws, tm, batch, q_scale):
    t, d = x2d.shape
    s = t // batch
    per_batch = s // tm
    full = lambda a: pl.BlockSpec(a.shape, lambda i: (0, 0))
    row = lambda w: pl.BlockSpec((tm, w), lambda i: (i, 0))
    tr = lambda w: pl.BlockSpec((1, w, tm), lambda i: (i // per_batch, 0, i % per_batch))
    return pl.pallas_call(
        functools.partial(_inproj_kernel, q_scale=q_scale),
        grid=(t // tm,),
        in_specs=[row(d), full(wg), full(wq), full(wk), full(wv), full(ws)],
        out_specs=[row(wg.shape[1]), tr(wq.shape[1]), row(wk.shape[1]), tr(wv.shape[1]), row(ws.shape[1])],
        out_shape=[jax.ShapeDtypeStruct((t, wg.shape[1]), BF16),
                   jax.ShapeDtypeStruct((batch, wq.shape[1], s), BF16),
                   jax.ShapeDtypeStruct((t, wk.shape[1]), BF16),
                   jax.ShapeDtypeStruct((batch, wv.shape[1], s), BF16),
                   jax.ShapeDtypeStruct((t, ws.shape[1]), F32)],
        compiler_params=pltpu.CompilerParams(dimension_semantics=("parallel",),
                                             vmem_limit_bytes=VMEM_LIMIT),
        name="inproj",
    )(x2d, wg, wq, wk, wv, ws)


def _gdn_kernel(q_ref, k_ref, v_ref, z_ref, hq_ref, hk_ref, hv_ref, s_ref, cw_ref, ab_ref, nw_ref,
                o_ref, state_ref, *, ts):
    c_len = GDN_CHUNK
    n_chunks = ts // c_len
    nh = GDN_HEADS
    i = pl.program_id(1)

    @pl.when(i == 0)
    def _():
        state_ref[...] = jnp.zeros_like(state_ref)

    row8 = lax.broadcasted_iota(I32, (SUBLANES, GDN_W), 0)
    first = i == 0

    def conv_silu(x_ref_, h_ref_, col0):
        x = x_ref_[0].astype(F32)
        hb = jnp.where(first, 0.0, h_ref_[0].astype(F32))
        w = cw_ref[:, col0:col0 + GDN_W]
        y = x * w[GDN_CONV - 1:GDN_CONV, :]
        for j in range(1, GDN_CONV):
            xr = pltpu.roll(x, j, 0)
            fix = pltpu.roll(hb, j, 0)
            top = jnp.where(row8 < j, fix, xr[0:SUBLANES])
            xr = jnp.concatenate([top, xr[SUBLANES:]], axis=0)
            y = y + xr * w[GDN_CONV - 1 - j:GDN_CONV - j, :]
        return y * _sigmoid(y)

    qa = conv_silu(q_ref, hq_ref, 0)
    ka = conv_silu(k_ref, hk_ref, GDN_W)
    va = conv_silu(v_ref, hv_ref, 2 * GDN_W)

    sg = s_ref[0]
    beta_all = _sigmoid(sg)
    sp_in = sg + ab_ref[1:2, :]
    softplus = jnp.maximum(sp_in, 0.0) + jnp.log(1.0 + jnp.exp(-jnp.abs(sp_in)))
    g_step = ab_ref[0:1, :] * softplus

    r = lax.broadcasted_iota(I32, (ts, ts), 0)
    c = lax.broadcasted_iota(I32, (ts, ts), 1)
    tri = jnp.where((r // c_len) == (c // c_len), jnp.where(c <= r, 1.0, 0.0), 0.0)
    gc = jnp.dot(tri, g_step, precision=HIGHEST, preferred_element_type=F32)
    gct = [gc[cc * c_len:(cc + 1) * c_len, :].T for cc in range(n_chunks)]

    ri = lax.broadcasted_iota(I32, (c_len, c_len), 0)
    ci = lax.broadcasted_iota(I32, (c_len, c_len), 1)
    causal = ci <= ri
    strict = ci < ri
    nw = nw_ref[...]

    pairs = [(h, cc) for h in range(nh) for cc in range(n_chunks)]
    qs, ks, kbs, vbs, kbes, decays, qds, kts, gls = [], [], [], [], [], [], [], [], []
    for h in range(nh):
        lo, hi = h * GDN_DK, (h + 1) * GDN_DK
        qh = qa[:, lo:hi]
        kh = ka[:, lo:hi]
        vh = va[:, lo:hi]
        qh = qh * (lax.rsqrt(jnp.sum(qh * qh, axis=-1, keepdims=True) + RMS_EPS) * (GDN_DK ** -0.5))
        kh = kh * lax.rsqrt(jnp.sum(kh * kh, axis=-1, keepdims=True) + RMS_EPS)
        beta = beta_all[:, h:h + 1]
        gcol_all = gc[:, nh + h:nh + h + 1]
        for cc in range(n_chunks):
            r0, r1 = cc * c_len, (cc + 1) * c_len
            qc, kc, vc = qh[r0:r1], kh[r0:r1], vh[r0:r1]
            bcol = beta[r0:r1]
            gcol = gcol_all[r0:r1]
            grow = gct[cc][nh + h:nh + h + 1, :]
            decay = jnp.where(causal, jnp.exp(jnp.where(causal, gcol - grow, 0.0)), 0.0)
            eg = jnp.exp(gcol)
            g_last = gcol[c_len - 1:c_len, :]
            kb = kc * bcol
            qs.append(qc.astype(BF16))
            ks.append(kc.astype(BF16))
            kbs.append(kb)
            vbs.append(vc * bcol)
            kbes.append(kb * eg)
            decays.append(decay)
            qds.append((qc * eg).astype(BF16))
            kts.append((kc * jnp.exp(g_last - gcol)).astype(BF16))
            gls.append(jnp.exp(g_last))

    n = len(pairs)
    ms = [-jnp.where(strict, _dot_nt(kbs[b].astype(BF16), ks[b]) * decays[b], 0.0) for b in range(n)]
    ais = [jnp.where(causal, _dot_nt(qs[b], ks[b]) * decays[b], 0.0).astype(BF16) for b in range(n)]
    ys = ms
    for _ in range(5):
        mb = [m.astype(BF16) for m in ms]
        ms = [_dot(mb[b], mb[b]) for b in range(n)]
        mb = [m.astype(BF16) for m in ms]
        ys = [ys[b] + ms[b] + _dot(ys[b].astype(BF16), mb[b]) for b in range(n)]
    yb = [y.astype(BF16) for y in ys]
    us = [vbs[b] + _dot(yb[b], vbs[b].astype(BF16)) for b in range(n)]
    ws = [(kbes[b] + _dot(yb[b], kbes[b].astype(BF16))).astype(BF16) for b in range(n)]

    states = [state_ref[h] for h in range(nh)]
    outs = [[None] * n_chunks for _ in range(nh)]
    for cc in range(n_chunks):
        idx = [h * n_chunks + cc for h in range(nh)]
        sb = [st.astype(BF16) for st in states]
        v_new = [us[idx[h]] - _dot(ws[idx[h]], sb[h]) for h in range(nh)]
        vnb = [v.astype(BF16) for v in v_new]
        for h in range(nh):
            outs[h][cc] = _dot(qds[idx[h]], sb[h]) + _dot(ais[idx[h]], vnb[h])
        states = [states[h] * gls[idx[h]] + _dot_tn(kts[idx[h]], vnb[h]) for h in range(nh)]
    for h in range(nh):
        lo, hi = h * GDN_DK, (h + 1) * GDN_DK
        state_ref[h] = states[h]
        o = jnp.concatenate(outs[h], axis=0)
        o = o * lax.rsqrt(jnp.mean(o * o, axis=-1, keepdims=True) + RMS_EPS) * nw
        z = z_ref[0, :, lo:hi].astype(F32)
        o_ref[0, :, lo:hi] = (o * (z * _sigmoid(z))).astype(o_ref.dtype)


def _gdn(g3, s3, conv_w, ab, nw, ts):
    b, s, _ = g3.shape
    per_block = ts // SUBLANES

    def blk(col):
        return pl.BlockSpec((1, ts, GDN_W), lambda bb, i: (bb, i, col))

    def halo(col):
        return pl.BlockSpec((1, SUBLANES, GDN_W),
                            lambda bb, i: (bb, jnp.maximum(i * per_block - 1, 0), col))

    return pl.pallas_call(
        functools.partial(_gdn_kernel, ts=ts),
        grid=(b, s // ts),
        in_specs=[blk(0), blk(1), blk(2), blk(3), halo(0), halo(1), halo(2),
                  pl.BlockSpec((1, ts, SMALL_W), lambda bb, i: (bb, i, 0)),
                  pl.BlockSpec(conv_w.shape, lambda bb, i: (0, 0)),
                  pl.BlockSpec(ab.shape, lambda bb, i: (0, 0)),
                  pl.BlockSpec(nw.shape, lambda bb, i: (0, 0))],
        out_specs=pl.BlockSpec((1, ts, GDN_W), lambda bb, i: (bb, i, 0)),
        out_shape=jax.ShapeDtypeStruct((b, s, GDN_W), BF16),
        scratch_shapes=[pltpu.VMEM((GDN_HEADS, GDN_DK, GDN_DV), F32)],
        compiler_params=pltpu.CompilerParams(dimension_semantics=("parallel", "arbitrary"),
                                             vmem_limit_bytes=VMEM_LIMIT),
        name="gdn",
    )(g3, g3, g3, g3, g3, g3, g3, s3, conv_w, ab, nw)


def _diff_kernel(lam_ref, qt_ref, k_ref, vt_ref, nw_ref, o_ref, m_ref, acc_ref, s_ref, cm_ref, *, tq, tk,
                 out_scale):
    i = pl.program_id(2)
    qt = qt_ref[0]
    rowi = lax.broadcasted_iota(I32, (2 * DIFF_DQK, tq), 0)
    zero = jnp.zeros_like(qt)
    qq = jnp.concatenate([jnp.where(rowi < DIFF_DQK, qt, zero),
                          jnp.where(rowi >= DIFF_DQK, qt, zero)], axis=1)
    m_ref[...] = jnp.full(m_ref.shape, NEG_BIG, F32)
    acc_ref[...] = jnp.zeros(acc_ref.shape, F32)
    ones_rows = jnp.ones((BF16_SUBLANES, tk), BF16)

    def scores(key_block, slot, diag):
        start = pl.multiple_of(key_block * tk, tk)
        s = _dot(k_ref[0, pl.ds(start, tk), :], qq)
        if diag is not None:
            kr = lax.broadcasted_iota(I32, (tk, 2 * tq), 0) + diag * tk
            qc = lax.broadcasted_iota(I32, (tk, 2 * tq), 1)
            qc = jnp.where(qc >= tq, qc - tq, qc)
            s = jnp.where(kr <= qc, s, NEG_BIG)
        s_ref[slot] = s
        cm_ref[slot] = jnp.max(s, axis=0, keepdims=True)

    def update(key_block, slot):
        start = pl.multiple_of(key_block * tk, tk)
        vtj = vt_ref[0, :, pl.ds(start, tk)]
        m_old = m_ref[...]
        m_new = jnp.maximum(m_old, cm_ref[slot])
        alpha = jnp.exp2(m_old - m_new)
        p = jnp.exp2(s_ref[slot] - m_new).astype(BF16)
        vt_ext = jnp.concatenate([vtj, ones_rows], axis=0)
        acc_ref[...] = alpha * acc_ref[...] + _dot(vt_ext, p)
        m_ref[...] = m_new

    base = 2 * i
    scores(base, 0, 0)
    scores(base + 1, 1, 1)
    update(base, 0)

    def pair(jj, carry):
        prev = jnp.where(jj == 0, base + 1, 2 * jj - 1)
        scores(2 * jj, 0, None)
        update(prev, 1)
        scores(2 * jj + 1, 1, None)
        update(2 * jj, 0)
        return carry

    def two_pairs(jq, carry):
        pair(2 * jq, carry)
        pair(2 * jq + 1, carry)
        return carry

    lax.fori_loop(0, i // 2, two_pairs, 0)

    @pl.when(i % 2 == 1)
    def _():
        pair(i - 1, 0)

    update(jnp.where(i == 0, base + 1, base - 1), 1)

    acc = acc_ref[...]
    o = acc[:DIFF_DV] / acc[DIFF_DV:DIFF_DV + 1]
    od = o[:, :tq] - lam_ref[0] * o[:, tq:]
    od = od * lax.rsqrt(jnp.mean(od * od, axis=0, keepdims=True) + RMS_EPS) * nw_ref[...] * out_scale
    o_ref[0] = od.T.astype(o_ref.dtype)


def _diff_attn(lam, qt, k3, vt, nw_col, tq, out_scale):
    b, s, _ = k3.shape
    nh = DIFF_HEADS
    tk = tq // 2
    return pl.pallas_call(
        functools.partial(_diff_kernel, tq=tq, tk=tk, out_scale=out_scale),
        grid=(b, nh, s // tq),
        in_specs=[pl.BlockSpec(memory_space=pltpu.SMEM),
                  pl.BlockSpec((1, LANES, tq), lambda bb, h, i: (bb, h, i)),
                  pl.BlockSpec((1, s, LANES), lambda bb, h, i: (bb, 0, h)),
                  pl.BlockSpec((1, LANES, s), lambda bb, h, i: (bb, h, 0)),
                  pl.BlockSpec(nw_col.shape, lambda bb, h, i: (0, 0))],
        out_specs=pl.BlockSpec((1, tq, LANES), lambda bb, h, i: (bb, i, h)),
        out_shape=jax.ShapeDtypeStruct((b, s, nh * DIFF_DV), BF16),
        scratch_shapes=[pltpu.VMEM((1, 2 * tq), F32), pltpu.VMEM((DIFF_DV + BF16_SUBLANES, 2 * tq), F32),
                        pltpu.VMEM((2, tk, 2 * tq), F32), pltpu.VMEM((2, 1, 2 * tq), F32)],
        compiler_params=pltpu.CompilerParams(dimension_semantics=("parallel", "parallel", "arbitrary"),
                                             vmem_limit_bytes=VMEM_LIMIT),
        name="diff_attn",
    )(lam, qt, k3, vt, nw_col)


def _memkv_kernel(mem_ref, wk_ref, wv_ref, kt_ref, v_ref, *, scale):
    mb = mem_ref[0].astype(BF16)
    k = _dot(mb, wk_ref[...])
    kt_ref[0] = (k.T * scale).astype(BF16)
    v_ref[0] = _dot(mb, wv_ref[...]).astype(BF16)


def _memkv(mem, wk, wv, scale):
    b, m, d = mem.shape
    return pl.pallas_call(
        functools.partial(_memkv_kernel, scale=scale),
        grid=(b,),
        in_specs=[pl.BlockSpec((1, m, d), lambda bb: (bb, 0, 0)),
                  pl.BlockSpec(wk.shape, lambda bb: (0, 0)),
                  pl.BlockSpec(wv.shape, lambda bb: (0, 0))],
        out_specs=[pl.BlockSpec((1, d, m), lambda bb: (bb, 0, 0)),
                   pl.BlockSpec((1, m, d), lambda bb: (bb, 0, 0))],
        out_shape=[jax.ShapeDtypeStruct((b, d, m), BF16), jax.ShapeDtypeStruct((b, m, d), BF16)],
        compiler_params=pltpu.CompilerParams(dimension_semantics=("parallel",),
                                             vmem_limit_bytes=VMEM_LIMIT),
        name="memkv",
    )(mem, wk, wv)


def _mid_kernel(x_ref, og_ref, od_ref, wo1_ref, wo2_ref, g1_ref, b1_ref, wq_ref, kt_ref, v_ref, wo_ref,
                g2_ref, b2_ref, rw_ref, rw1_ref, rb_ref, x2_ref, x2p_ref, route_ref, gate_ref, cnt_ref, run_ref):
    i = pl.program_id(0)

    @pl.when(i == 0)
    def _():
        run_ref[...] = jnp.zeros_like(run_ref)

    tm, d = x_ref.shape
    sub = tm // MID_SUBBLOCKS
    spans = [(k * sub, (k + 1) * sub) for k in range(MID_SUBBLOCKS)]
    dh = d // XA_HEADS

    h = [_dot(og_ref[a:b, :], wo1_ref[...]) + _dot(od_ref[a:b, :], wo2_ref[...]) for a, b in spans]
    x1 = [_layer_norm(DEEPNORM_ALPHA * x_ref[a:b, :] + hk, g1_ref[...], b1_ref[...])
          for (a, b), hk in zip(spans, h)]
    q = [_dot(xk.astype(BF16), wq_ref[...]).astype(BF16) for xk in x1]
    heads = [[] for _ in spans]
    for hh in range(XA_HEADS):
        lo, hi = hh * dh, (hh + 1) * dh
        s = [_dot(qk[:, lo:hi], kt_ref[0, lo:hi, :]) for qk in q]
        p = [jnp.exp(sk - jnp.max(sk, axis=-1, keepdims=True)) for sk in s]
        p = [(pk / jnp.sum(pk, axis=-1, keepdims=True)).astype(BF16) for pk in p]
        for k, pk in enumerate(p):
            heads[k].append(_dot(pk, v_ref[0, :, lo:hi]))
    o = [jnp.concatenate(hk, axis=-1).astype(BF16) for hk in heads]
    h2 = [_dot(ok, wo_ref[...]) for ok in o]
    x2 = [_layer_norm(DEEPNORM_ALPHA * xk + hk, g2_ref[...], b2_ref[...]) for xk, hk in zip(x1, h2)]

    x_hi = [xk.astype(BF16) for xk in x2]
    x_lo = [(xk - hk.astype(F32)).astype(BF16) for xk, hk in zip(x2, x_hi)]
    r1 = [_dot(hk, rw_ref[...]) for hk in x_hi]
    r2 = [_dot(lk, rw1_ref[...]) for lk in x_lo]
    logits = [a1 + pltpu.roll(a1, LANES - N_EXPERTS, 1) + a2 + rb_ref[...] for a1, a2 in zip(r1, r2)]

    lane = lax.broadcasted_iota(I32, (sub, LANES), 1)
    rr = lax.broadcasted_iota(I32, (sub, sub), 0)
    cc = lax.broadcasted_iota(I32, (sub, sub), 1)
    tri = jnp.where(cc < rr, 1.0, 0.0).astype(BF16)
    run = run_ref[...]
    for (a, b), xk, lgk in zip(spans, x2, logits):
        x2_ref[a:b, :] = xk
        packed = _pack_bf16_pairs(xk)
        for piece in range(x2p_ref.shape[0]):
            x2p_ref[piece, a:b, :] = packed[:, piece * SC_ROW_WORDS:(piece + 1) * SC_ROW_WORDS]
        lg = jnp.where(lane < N_EXPERTS, lgk, NEG_BIG)
        sel = jnp.zeros((sub, LANES), F32)
        vals, hots, idxs = [], [], []
        for _ in range(TOP_K):
            mx = jnp.max(lg, axis=-1, keepdims=True)
            idx = jnp.min(jnp.where(lg == mx, lane, LANES), axis=-1, keepdims=True)
            hot = lane == idx
            vals.append(mx)
            idxs.append(idx)
            hots.append(hot)
            lg = jnp.where(hot, NEG_BIG, lg)
            sel = sel + jnp.where(hot, 1.0, 0.0)
        ex = [jnp.exp(v - vals[0]) for v in vals]
        den = ex[0] + ex[1] + ex[2] + ex[3]
        before = _dot(tri, sel.astype(BF16)) + run
        route = jnp.zeros((sub, LANES), F32)
        gates = jnp.zeros((sub, LANES), F32)
        for kk in range(TOP_K):
            rank = jnp.sum(jnp.where(hots[kk], before, 0.0), axis=-1, keepdims=True)
            route = jnp.where(lane == kk, idxs[kk].astype(F32), route)
            route = jnp.where(lane == TOP_K + kk, rank, route)
            gates = jnp.where(lane == kk, ex[kk] / den, gates)
        route_ref[:, a:b] = route.T[:2 * TOP_K].astype(I32)
        gate_ref[a:b, :] = gates
        run = run + jnp.sum(sel, axis=0, keepdims=True)
    run_ref[...] = run
    cnt_ref[...] = run.astype(I32)


def _mid(x2d, og, od, wo1, wo2, g1, b1, wq, kt, v, wo, g2, b2, rw, rw1, rb, tm, rows_per_batch):
    t, d = x2d.shape
    blocks_per_batch = rows_per_batch // tm
    pieces = (d // 2) // SC_ROW_WORDS
    row = lambda w: pl.BlockSpec((tm, w), lambda i: (i, 0))
    full = lambda a: pl.BlockSpec(a.shape, lambda i: (0, 0))
    per_batch = lambda a: pl.BlockSpec((1,) + a.shape[1:], lambda i: (i // blocks_per_batch, 0, 0))
    return pl.pallas_call(
        _mid_kernel,
        grid=(t // tm,),
        in_specs=[row(d), row(og.shape[1]), row(od.shape[1]), full(wo1), full(wo2), full(g1), full(b1),
                  full(wq), per_batch(kt), per_batch(v), full(wo), full(g2), full(b2), full(rw), full(rw1), full(rb)],
        out_specs=[row(d), pl.BlockSpec((pieces, tm, SC_ROW_WORDS), lambda i: (0, i, 0)),
                   pl.BlockSpec((2 * TOP_K, tm), lambda i: (0, i)), row(LANES),
                   pl.BlockSpec((1, LANES), lambda i: (0, 0))],
        out_shape=[jax.ShapeDtypeStruct((t, d), F32), jax.ShapeDtypeStruct((pieces, t, SC_ROW_WORDS), U32),
                   jax.ShapeDtypeStruct((2 * TOP_K, t), I32), jax.ShapeDtypeStruct((t, LANES), F32),
                   jax.ShapeDtypeStruct((1, LANES), I32)],
        scratch_shapes=[pltpu.VMEM((1, LANES), F32)],
        compiler_params=pltpu.CompilerParams(dimension_semantics=("arbitrary",),
                                             vmem_limit_bytes=VMEM_LIMIT),
        name="mid",
    )(x2d, og, od, wo1, wo2, g1, b1, wq, kt, v, wo, g2, b2, rw, rw1, rb)


def _sc_scatter_rows(rows, idx4, n_out):
    t, d = rows.shape
    mesh = plsc.VectorSubcoreMesh(core_axis_name="c", subcore_axis_name="s")

    @functools.partial(pl.kernel, out_type=jax.ShapeDtypeStruct((n_out, d), rows.dtype), mesh=mesh)
    def scatter_kernel(x_hbm, i0_hbm, i1_hbm, i2_hbm, i3_hbm, o_hbm):
        def body(x_vmem, i0, i1, i2, i3):
            for iv in (i0, i1, i2, i3):
                pltpu.sync_copy(x_vmem, o_hbm.at[iv.at[0]])

        ispec = pl.BlockSpec((1, SC_WINDOW), lambda i: (0, i))
        pltpu.emit_pipeline(
            body,
            grid=(t // SC_WINDOW,),
            in_specs=[pl.BlockSpec((SC_WINDOW, d), lambda i: (i, 0)), ispec, ispec, ispec, ispec],
            out_specs=[],
            core_axis_name=("c", "s"),
            dimension_semantics=(pltpu.PARALLEL,),
        )(x_hbm, i0_hbm, i1_hbm, i2_hbm, i3_hbm)

    return scatter_kernel(rows, *[idx4[k:k + 1] for k in range(TOP_K)])


def _sc_gather_rows(table, idx):
    n = idx.shape[0]
    d = table.shape[1]
    mesh = plsc.VectorSubcoreMesh(core_axis_name="c", subcore_axis_name="s")

    @functools.partial(pl.kernel, out_type=jax.ShapeDtypeStruct((n, d), table.dtype), mesh=mesh)
    def gather_kernel(x_hbm, i_hbm, o_hbm):
        def body(i_vmem, o_vmem):
            pltpu.sync_copy(x_hbm.at[i_vmem.at[0]], o_vmem)

        pltpu.emit_pipeline(
            body,
            grid=(n // SC_WINDOW,),
            in_specs=[pl.BlockSpec((1, SC_WINDOW), lambda i: (0, i))],
            out_specs=[pl.BlockSpec((SC_WINDOW, d), lambda i: (i, 0))],
            core_axis_name=("c", "s"),
            dimension_semantics=(pltpu.PARALLEL,),
        )(i_hbm, o_hbm)

    return gather_kernel(table, idx.reshape(1, n))


def _moe_kernel(be_ref, nu_ref, nx_ref, x_ref, wg_hbm, bg_ref, wu_hbm, bu_ref, wd_hbm, bd_ref, y_ref,
                stage_ref, wgb_ref, wub_ref, wdb_ref, sem):
    i = pl.program_id(0)
    used = i < nu_ref[0]
    expert = be_ref[i]
    prev = be_ref[jnp.maximum(i - 1, 0)]
    changed = jnp.logical_and(used, jnp.logical_or(i == 0, expert != prev))

    def weight_copies(e):
        return [pltpu.make_async_copy(w.at[e], stage_ref.at[k], sem.at[k])
                for k, w in enumerate((wg_hbm, wu_hbm, wd_hbm))]

    @pl.when(changed)
    def _():
        @pl.when(i == 0)
        def _():
            for cp in weight_copies(expert):
                cp.start()

        for cp in weight_copies(expert):
            cp.wait()
        wgb_ref[...] = stage_ref[0].astype(BF16)
        wub_ref[...] = stage_ref[1].astype(BF16)
        wdb_ref[...] = stage_ref[2].astype(BF16)
        nxt = nx_ref[i]

        @pl.when(nxt >= 0)
        def _():
            for cp in weight_copies(nxt):
                cp.start()

    @pl.when(used)
    def _():
        halves = [_unpack_bf16_pairs(x_ref[piece]) for piece in range(x_ref.shape[0])]
        x = jnp.concatenate([h[0] for h in halves] + [h[1] for h in halves], axis=-1).astype(BF16)
        hg = _dot(x, wgb_ref[...]) + bg_ref[0]
        hl = _dot(x, wub_ref[...]) + bu_ref[0]
        hg = jnp.minimum(hg, SWIGLU_LIMIT)
        hl = jnp.clip(hl, -SWIGLU_LIMIT, SWIGLU_LIMIT)
        act = hg * _sigmoid(SWIGLU_ALPHA * hg) * (hl + 1.0)
        y = _dot(act.astype(BF16), wdb_ref[...]) + bd_ref[0]
        packed = _pack_bf16_pairs(y)
        for piece in range(y_ref.shape[0]):
            y_ref[piece] = packed[:, piece * SC_ROW_WORDS:(piece + 1) * SC_ROW_WORDS]

    @pl.when(jnp.logical_not(used))
    def _():
        y_ref[...] = jnp.zeros_like(y_ref)


def _moe(block_e, n_used, next_e, xs, wg, bg, wu, bu, wd, bd, tm):
    pieces, p, dp = xs.shape
    d, dff = wg.shape[1], wg.shape[2]
    assert d == dff, "the shared f32 staging buffer assumes square expert matrices"
    n_blocks = p // tm
    bspec = lambda a: pl.BlockSpec((1,) + a.shape[1:], lambda i, be, nu, nx: (be[i], 0, 0))
    hbm = pl.BlockSpec(memory_space=pl.ANY)
    rows = pl.BlockSpec((pieces, tm, dp), lambda i, be, nu, nx: (0, i, 0))
    grid_spec = pltpu.PrefetchScalarGridSpec(
        num_scalar_prefetch=3,
        grid=(n_blocks,),
        in_specs=[rows, hbm, bspec(bg), hbm, bspec(bu), hbm, bspec(bd)],
        out_specs=rows,
        scratch_shapes=[pltpu.VMEM((3, d, dff), F32),
                        pltpu.VMEM((d, dff), BF16), pltpu.VMEM((d, dff), BF16), pltpu.VMEM((dff, d), BF16),
                        pltpu.SemaphoreType.DMA((3,))],
    )
    return pl.pallas_call(
        _moe_kernel,
        grid_spec=grid_spec,
        out_shape=jax.ShapeDtypeStruct((pieces, p, dp), U32),
        compiler_params=pltpu.CompilerParams(dimension_semantics=("arbitrary",),
                                             vmem_limit_bytes=VMEM_LIMIT),
        name="moe_ffn",
    )(block_e, n_used, next_e, xs, wg, bg, wu, bu, wd, bd)


def _final_kernel(x_ref, y4_ref, gate_ref, g_ref, b_ref, o_ref):
    pieces = y4_ref.shape[0]
    gates = gate_ref[...]
    lows = [0.0] * pieces
    highs = [0.0] * pieces
    for kk in range(TOP_K):
        gk = gates[:, kk:kk + 1]
        for piece in range(pieces):
            lo, hi = _unpack_bf16_pairs(y4_ref[piece, kk])
            lows[piece] = lows[piece] + gk * lo
            highs[piece] = highs[piece] + gk * hi
    y = jnp.concatenate(lows + highs, axis=-1)
    o_ref[...] = _layer_norm(DEEPNORM_ALPHA * x_ref[...] + y, g_ref[...], b_ref[...])


def _final(x2, y4, gates, g, b, tm):
    t, d = x2.shape
    row = lambda w: pl.BlockSpec((tm, w), lambda i: (i, 0))
    full = lambda a: pl.BlockSpec(a.shape, lambda i: (0, 0))
    return pl.pallas_call(
        _final_kernel,
        grid=(t // tm,),
        in_specs=[row(d), pl.BlockSpec(y4.shape[:2] + (tm, y4.shape[3]), lambda i: (0, 0, i, 0)),
                  row(gates.shape[1]), full(g), full(b)],
        out_specs=row(d),
        out_shape=jax.ShapeDtypeStruct((t, d), F32),
        compiler_params=pltpu.CompilerParams(dimension_semantics=("parallel",),
                                             vmem_limit_bytes=VMEM_LIMIT),
        name="final_ln",
    )(x2, y4, gates, g, b)


def _layout(route, counts, tm):
    t = route.shape[1]
    experts = route[:TOP_K]
    ranks = route[TOP_K:]
    counts = counts[0, :N_EXPERTS]
    padded = (counts + tm - 1) // tm * tm
    pad_end = jnp.cumsum(padded)
    pad_start = pad_end - padded
    eids = jnp.arange(N_EXPERTS, dtype=I32)[:, None, None]
    dest = ranks + jnp.sum(jnp.where(experts[None] == eids, pad_start[:, None, None], 0), axis=0)
    n_blocks = t * TOP_K // tm + N_EXPERTS
    starts = jnp.arange(n_blocks, dtype=I32) * tm
    block_e = jnp.minimum(jnp.sum((pad_end[None, :] <= starts[:, None]).astype(I32), axis=-1), N_EXPERTS - 1)
    n_used = (pad_end[-1] // tm).astype(I32).reshape(1)
    e_row = jnp.arange(N_EXPERTS, dtype=I32)
    later = (e_row[None, :] > e_row[:, None]) & (padded[None, :] > 0)
    next_of = jnp.min(jnp.where(later, e_row[None, :], N_EXPERTS), axis=1)
    next_of = jnp.where(next_of == N_EXPERTS, -1, next_of)
    next_e = jnp.sum(jnp.where(block_e[:, None] == e_row[None, :], next_of[None, :], 0), axis=1)
    return dest.astype(I32), block_e.astype(I32), n_used, next_e.astype(I32), n_blocks


def kernel(x, mem, w_in, gdn_conv_w, gdn_a_log, gdn_dt_bias, gdn_norm_w, diff_lq1, diff_lk1, diff_lq2, diff_lk2,
           diff_norm_w, w_out, ln1_g, ln1_b, xa_wq, xa_wk, xa_wv, xa_wo, ln2_g, ln2_b, router_w, router_b,
           exp_w_gate, exp_b_gate, exp_w_up, exp_b_up, exp_w_down, exp_b_down, ln3_g, ln3_b):
    b, s, d = x.shape
    t = b * s
    x2d = x.reshape(t, d)
    for l in range(DEPTH):
        lambda_init = 0.8 - 0.6 * math.exp(-0.3 * l)
        n_g = 4 * GDN_W
        w = w_in[l]
        wg = w[:, :n_g].astype(BF16)
        small = w[:, n_g:n_g + 2 * GDN_HEADS]
        ws = jnp.pad(small, ((0, 0), (0, SMALL_W - 2 * GDN_HEADS))).astype(BF16)
        d0 = n_g + 2 * GDN_HEADS
        wdq = w[:, d0:d0 + DIFF_W].astype(BF16)
        wdk = w[:, d0 + DIFF_W:d0 + 2 * DIFF_W].astype(BF16)
        wdv = w[:, d0 + 2 * DIFF_W:d0 + 3 * DIFF_W].astype(BF16)
        conv_w = gdn_conv_w[l].reshape(GDN_CONV, 3 * GDN_W).astype(F32)
        ab = jnp.zeros((2, SMALL_W), F32)
        ab = ab.at[0, GDN_HEADS:2 * GDN_HEADS].set(-jnp.exp(gdn_a_log[l].astype(F32)))
        ab = ab.at[1, GDN_HEADS:2 * GDN_HEADS].set(gdn_dt_bias[l].astype(F32))
        lam = (jnp.exp(jnp.sum(diff_lq1[l].astype(F32) * diff_lk1[l].astype(F32)))
               - jnp.exp(jnp.sum(diff_lq2[l].astype(F32) * diff_lk2[l].astype(F32))) + lambda_init).reshape(1)

        g2d, dqt, dk2d, dvt, s2d = _inproj(x2d, wg, wdq, wdk, wdv, ws, tm=512, batch=b,
                                           q_scale=LOG2E * DIFF_DQK ** -0.5)
        o_gdn = _gdn(g2d.reshape(b, s, n_g), s2d.reshape(b, s, SMALL_W), conv_w, ab,
                     gdn_norm_w[l].reshape(1, GDN_DV).astype(F32), ts=256)
        o_diff = _diff_attn(lam, dqt, dk2d.reshape(b, s, DIFF_W), dvt,
                            diff_norm_w[l].reshape(DIFF_DV, 1).astype(F32), tq=512, out_scale=1.0 - lambda_init)

        dh = d // XA_HEADS
        kt, v = _memkv(mem, xa_wk[l].astype(BF16), xa_wv[l].astype(BF16), scale=dh ** -0.5)
        wo_mix = w_out[l].astype(BF16)
        rw_f = router_w[l].astype(F32)
        rw_hi = rw_f.astype(BF16)
        rw_lo = (rw_f - rw_hi.astype(F32)).astype(BF16)
        rw = jnp.pad(jnp.concatenate([rw_hi, rw_lo], axis=1), ((0, 0), (0, LANES - 2 * N_EXPERTS)))
        rw1 = jnp.pad(rw_hi, ((0, 0), (0, LANES - N_EXPERTS)))
        rb = jnp.pad(router_b[l].astype(F32), (0, LANES - N_EXPERTS)).reshape(1, LANES)
        row = lambda a: a.reshape(1, -1).astype(F32)
        x2, x2p, route, gates, counts = _mid(
            x2d, o_gdn.reshape(t, GDN_W), o_diff.reshape(t, DIFF_HEADS * DIFF_DV),
            wo_mix[:GDN_W], wo_mix[GDN_W:], row(ln1_g[l]), row(ln1_b[l]),
            xa_wq[l].astype(BF16), kt, v, xa_wo[l].astype(BF16), row(ln2_g[l]), row(ln2_b[l]),
            rw, rw1, rb, tm=512, rows_per_batch=s)

        tm_moe = 256
        dest, block_e, n_used, next_e, n_blocks = _layout(route, counts, tm_moe)
        pieces = x2p.shape[0]
        p_rows = n_blocks * tm_moe
        offs = (jnp.arange(pieces, dtype=I32) * p_rows)[:, None]
        idx_scatter = (dest[:, None, :] + offs[None]).reshape(TOP_K, pieces * t)
        xs = _sc_scatter_rows(x2p.reshape(pieces * t, SC_ROW_WORDS), idx_scatter,
                              pieces * p_rows).reshape(pieces, p_rows, SC_ROW_WORDS)
        ys = _moe(block_e, n_used, next_e, xs,
                  exp_w_gate[l], exp_b_gate[l].reshape(N_EXPERTS, 1, -1), exp_w_up[l],
                  exp_b_up[l].reshape(N_EXPERTS, 1, -1), exp_w_down[l], exp_b_down[l].reshape(N_EXPERTS, 1, -1),
                  tm=tm_moe)
        idx_gather = (dest.reshape(1, TOP_K * t) + offs).reshape(pieces * TOP_K * t)
        y4 = _sc_gather_rows(ys.reshape(pieces * p_rows, SC_ROW_WORDS),
                             idx_gather).reshape(pieces, TOP_K, t, SC_ROW_WORDS)
        x2d = _final(x2, y4, gates, row(ln3_g[l]), row(ln3_b[l]), tm=512)
    return x2d.reshape(b, s, d)
```

```python
import functools
import math

import jax
import jax.numpy as jnp
from jax import lax
from jax.experimental import pallas as pl
from jax.experimental.pallas import tpu as pltpu
from jax.experimental.pallas import tpu_sc as plsc

F32 = jnp.float32
BF16 = jnp.bfloat16
U32 = jnp.uint32
I32 = jnp.int32
HIGHEST = lax.Precision.HIGHEST

GDN_HEADS = 4
GDN_DK = 128
GDN_DV = 128
GDN_CONV = 4
GDN_CHUNK = 64
DIFF_HEADS = 4
DIFF_DQK = 64
DIFF_DV = 128
XA_HEADS = 4
N_EXPERTS = 32
TOP_K = 4
SWIGLU_ALPHA = 1.702
SWIGLU_LIMIT = 7.0
LN_EPS = 1e-5
RMS_EPS = 1e-6
DEPTH = 1
DEEPNORM_ALPHA = (2 * DEPTH) ** 0.25

GDN_W = GDN_HEADS * GDN_DK
DIFF_W = DIFF_HEADS * 2 * DIFF_DQK
SMALL_W = 128

LANES = 128
SUBLANES = 8
BF16_SUBLANES = 16
VMEM_LIMIT = 56 * 1024 * 1024
SC_WINDOW = 128
SC_ROW_WORDS = 256
DIFF_HEADS_PER_STEP = 2
MID_SUBBLOCKS = 4

NEG_BIG = -1e30
LOG2E = 1.4426950408889634


def _dot(a, b):
    return jnp.dot(a, b, preferred_element_type=F32)


def _dot_nt(a, b):
    return lax.dot_general(a, b, (((1,), (1,)), ((), ())), preferred_element_type=F32)


def _dot_tn(a, b):
    return lax.dot_general(a, b, (((0,), (0,)), ((), ())), preferred_element_type=F32)


def _layer_norm(y, g, b):
    mu = jnp.mean(y, axis=-1, keepdims=True)
    d = y - mu
    var = jnp.mean(d * d, axis=-1, keepdims=True)
    return d * lax.rsqrt(var + LN_EPS) * g + b


def _sigmoid(x):
    return 1.0 / (1.0 + jnp.exp(-x))


def _pack_bf16_pairs(x):
    n = x.shape[1] // 2
    bits = pltpu.bitcast(x.astype(BF16).astype(F32), U32)
    return (bits[:, :n] >> 16) | bits[:, n:]


def _unpack_bf16_pairs(w):
    lo = pltpu.bitcast(w << 16, F32)
    hi = pltpu.bitcast(w & jnp.uint32(0xFFFF0000), F32)
    return lo, hi


def _inproj_kernel(x_ref, wg_ref, wq_ref, wk_ref, wv_ref, ws_ref, cw_ref, g_ref, qt_ref, k_ref, vt_ref, s_ref,
                   tail_ref, *, q_scale, per_batch):
    i = pl.program_id(0)
    tm = x_ref.shape[0]
    n_conv = 3 * GDN_W

    @pl.when(i == 0)
    def _():
        tail_ref[...] = jnp.zeros_like(tail_ref)

    xb = x_ref[...].astype(BF16)
    first = i % per_batch == 0
    slab = GDN_W
    row8 = lax.broadcasted_iota(I32, (SUBLANES, slab), 0)

    def proj_z_qt():
        g_ref[:, n_conv:] = _dot(xb, wg_ref[:, n_conv:]).astype(BF16)
        qt_ref[0] = (_dot(xb, wq_ref[...]) * q_scale).T.astype(BF16)

    def proj_k():
        k_ref[...] = _dot(xb, wk_ref[...]).astype(BF16)

    def proj_vt_s():
        vt_ref[0] = _dot(xb, wv_ref[...]).T.astype(BF16)
        s_ref[...] = _dot(xb, ws_ref[...])

    others = [proj_z_qt, proj_k, proj_vt_s]
    assert n_conv // slab == len(others)

    for c0 in range(0, n_conv, slab):
        pre = _dot(xb, wg_ref[:, c0:c0 + slab])
        others[c0 // slab]()
        halo = jnp.where(first, 0.0, tail_ref[:, c0:c0 + slab])
        tail_ref[:, c0:c0 + slab] = pre[tm - SUBLANES:, :]
        w = cw_ref[:, c0:c0 + slab]
        y = pre * w[GDN_CONV - 1:GDN_CONV, :]
        for j in range(1, GDN_CONV):
            xr = pltpu.roll(pre, j, 0)
            top = jnp.where(row8 < j, pltpu.roll(halo, j, 0), xr[0:SUBLANES])
            xr = jnp.concatenate([top, xr[SUBLANES:]], axis=0)
            y = y + xr * w[GDN_CONV - 1 - j:GDN_CONV - j, :]
        y = y * _sigmoid(y)
        for lo in range(0, slab, GDN_DK):
            yh = y[:, lo:lo + GDN_DK]
            if c0 < GDN_W:
                yh = yh * (lax.rsqrt(jnp.sum(yh * yh, axis=-1, keepdims=True) + RMS_EPS) * (GDN_DK ** -0.5))
            elif c0 < 2 * GDN_W:
                yh = yh * lax.rsqrt(jnp.sum(yh * yh, axis=-1, keepdims=True) + RMS_EPS)
            g_ref[:, c0 + lo:c0 + lo + GDN_DK] = yh.astype(BF16)


def _inproj(x2d, wg, wq, wk, wv, ws, conv_w, tm, batch, q_scale):
    t, d = x2d.shape
    s = t // batch
    per_batch = s // tm
    full = lambda a: pl.BlockSpec(a.shape, lambda i: (0, 0))
    row = lambda w: pl.BlockSpec((tm, w), lambda i: (i, 0))
    tr = lambda w: pl.BlockSpec((1, w, tm), lambda i: (i // per_batch, 0, i % per_batch))
    return pl.pallas_call(
        functools.partial(_inproj_kernel, q_scale=q_scale, per_batch=per_batch),
        grid=(t // tm,),
        in_specs=[row(d), full(wg), full(wq), full(wk), full(wv), full(ws), full(conv_w)],
        out_specs=[row(wg.shape[1]), tr(wq.shape[1]), row(wk.shape[1]), tr(wv.shape[1]), row(ws.shape[1])],
        out_shape=[jax.ShapeDtypeStruct((t, wg.shape[1]), BF16),
                   jax.ShapeDtypeStruct((batch, wq.shape[1], s), BF16),
                   jax.ShapeDtypeStruct((t, wk.shape[1]), BF16),
                   jax.ShapeDtypeStruct((batch, wv.shape[1], s), BF16),
                   jax.ShapeDtypeStruct((t, ws.shape[1]), F32)],
        scratch_shapes=[pltpu.VMEM((SUBLANES, conv_w.shape[1]), F32)],
        compiler_params=pltpu.CompilerParams(dimension_semantics=("arbitrary",),
                                             vmem_limit_bytes=VMEM_LIMIT),
        name="inproj",
    )(x2d, wg, wq, wk, wv, ws, conv_w)


def _gdn_kernel(q_ref, k_ref, v_ref, z_ref, s_ref, ab_ref, nw_ref, o_ref, state_ref, *, ts):
    c_len = GDN_CHUNK
    n_chunks = ts // c_len
    nh = GDN_HEADS
    i = pl.program_id(1)

    @pl.when(i == 0)
    def _():
        state_ref[...] = jnp.zeros_like(state_ref)

    qa = q_ref[0].astype(F32)
    ka = k_ref[0].astype(F32)
    va = v_ref[0].astype(F32)

    sg = s_ref[0]
    beta_all = _sigmoid(sg)
    sp_in = sg + ab_ref[1:2, :]
    softplus = jnp.maximum(sp_in, 0.0) + jnp.log(1.0 + jnp.exp(-jnp.abs(sp_in)))
    g_step = ab_ref[0:1, :] * softplus

    r = lax.broadcasted_iota(I32, (ts, ts), 0)
    c = lax.broadcasted_iota(I32, (ts, ts), 1)
    tri = jnp.where((r // c_len) == (c // c_len), jnp.where(c <= r, 1.0, 0.0), 0.0)
    gc = jnp.dot(tri, g_step, precision=HIGHEST, preferred_element_type=F32)
    gct = [gc[cc * c_len:(cc + 1) * c_len, :].T for cc in range(n_chunks)]

    ri = lax.broadcasted_iota(I32, (c_len, c_len), 0)
    ci = lax.broadcasted_iota(I32, (c_len, c_len), 1)
    causal = ci <= ri
    strict = ci < ri
    nw = nw_ref[...]

    pairs = [(h, cc) for h in range(nh) for cc in range(n_chunks)]
    qs, ks, kbs, vbs, kbes, decays, qds, kts, gls = [], [], [], [], [], [], [], [], []
    for h in range(nh):
        lo, hi = h * GDN_DK, (h + 1) * GDN_DK
        qh = qa[:, lo:hi]
        kh = ka[:, lo:hi]
        vh = va[:, lo:hi]
        beta = beta_all[:, h:h + 1]
        gcol_all = gc[:, nh + h:nh + h + 1]
        for cc in range(n_chunks):
            r0, r1 = cc * c_len, (cc + 1) * c_len
            qc, kc, vc = qh[r0:r1], kh[r0:r1], vh[r0:r1]
            bcol = beta[r0:r1]
            gcol = gcol_all[r0:r1]
            grow = gct[cc][nh + h:nh + h + 1, :]
            decay = jnp.where(causal, jnp.exp(jnp.where(causal, gcol - grow, 0.0)), 0.0)
            eg = jnp.exp(gcol)
            g_last = gcol[c_len - 1:c_len, :]
            kb = kc * bcol
            qs.append(qc.astype(BF16))
            ks.append(kc.astype(BF16))
            kbs.append(kb)
            vbs.append(vc * bcol)
            kbes.append(kb * eg)
            decays.append(decay)
            qds.append((qc * eg).astype(BF16))
            kts.append((kc * jnp.exp(g_last - gcol)).astype(BF16))
            gls.append(jnp.exp(g_last))

    n = len(pairs)
    ms = [-jnp.where(strict, _dot_nt(kbs[b].astype(BF16), ks[b]) * decays[b], 0.0) for b in range(n)]
    ais = [jnp.where(causal, _dot_nt(qs[b], ks[b]) * decays[b], 0.0).astype(BF16) for b in range(n)]
    ys = ms
    for _ in range(5):
        mb = [m.astype(BF16) for m in ms]
        ms = [_dot(mb[b], mb[b]) for b in range(n)]
        mb = [m.astype(BF16) for m in ms]
        ys = [ys[b] + ms[b] + _dot(ys[b].astype(BF16), mb[b]) for b in range(n)]
    yb = [y.astype(BF16) for y in ys]
    us = [vbs[b] + _dot(yb[b], vbs[b].astype(BF16)) for b in range(n)]
    ws = [(kbes[b] + _dot(yb[b], kbes[b].astype(BF16))).astype(BF16) for b in range(n)]

    states = [state_ref[h] for h in range(nh)]
    outs = [[None] * n_chunks for _ in range(nh)]
    for cc in range(n_chunks):
        idx = [h * n_chunks + cc for h in range(nh)]
        sb = [st.astype(BF16) for st in states]
        v_new = [us[idx[h]] - _dot(ws[idx[h]], sb[h]) for h in range(nh)]
        vnb = [v.astype(BF16) for v in v_new]
        for h in range(nh):
            outs[h][cc] = _dot(qds[idx[h]], sb[h]) + _dot(ais[idx[h]], vnb[h])
        states = [states[h] * gls[idx[h]] + _dot_tn(kts[idx[h]], vnb[h]) for h in range(nh)]
    for h in range(nh):
        lo, hi = h * GDN_DK, (h + 1) * GDN_DK
        state_ref[h] = states[h]
        o = jnp.concatenate(outs[h], axis=0)
        o = o * lax.rsqrt(jnp.mean(o * o, axis=-1, keepdims=True) + RMS_EPS) * nw
        z = z_ref[0, :, lo:hi].astype(F32)
        o_ref[0, :, lo:hi] = (o * (z * _sigmoid(z))).astype(o_ref.dtype)


def _gdn(g3, s3, ab, nw, ts):
    b, s, _ = g3.shape

    def blk(col):
        return pl.BlockSpec((1, ts, GDN_W), lambda bb, i: (bb, i, col))

    return pl.pallas_call(
        functools.partial(_gdn_kernel, ts=ts),
        grid=(b, s // ts),
        in_specs=[blk(0), blk(1), blk(2), blk(3),
                  pl.BlockSpec((1, ts, SMALL_W), lambda bb, i: (bb, i, 0)),
                  pl.BlockSpec(ab.shape, lambda bb, i: (0, 0)),
                  pl.BlockSpec(nw.shape, lambda bb, i: (0, 0))],
        out_specs=pl.BlockSpec((1, ts, GDN_W), lambda bb, i: (bb, i, 0)),
        out_shape=jax.ShapeDtypeStruct((b, s, GDN_W), BF16),
        scratch_shapes=[pltpu.VMEM((GDN_HEADS, GDN_DK, GDN_DV), F32)],
        compiler_params=pltpu.CompilerParams(dimension_semantics=("parallel", "arbitrary"),
                                             vmem_limit_bytes=VMEM_LIMIT),
        name="gdn",
    )(g3, g3, g3, g3, s3, ab, nw)


def _diff_kernel(lam_ref, qt_ref, k_ref, vt_ref, nw_ref, o_ref, m_ref, acc_ref, s_ref, cm_ref, *, tq, tk,
                 out_scale):
    i = pl.program_id(2)
    heads = range(m_ref.shape[0])
    dq2 = 2 * DIFF_DQK
    rowi = lax.broadcasted_iota(I32, (dq2, tq), 0)
    qq = []
    for hd in heads:
        qt = qt_ref[0, hd * dq2:(hd + 1) * dq2, :]
        zero = jnp.zeros_like(qt)
        qq.append(jnp.concatenate([jnp.where(rowi < DIFF_DQK, qt, zero),
                                   jnp.where(rowi >= DIFF_DQK, qt, zero)], axis=1))
    m_ref[...] = jnp.full(m_ref.shape, NEG_BIG, F32)
    acc_ref[...] = jnp.zeros(acc_ref.shape, F32)
    ones_rows = jnp.ones((BF16_SUBLANES, tk), BF16)

    def scores(key_block, slot, diag):
        start = pl.multiple_of(key_block * tk, tk)
        for hd in heads:
            s = _dot(k_ref[0, pl.ds(start, tk), hd * dq2:(hd + 1) * dq2], qq[hd])
            if diag is not None:
                kr = lax.broadcasted_iota(I32, (tk, 2 * tq), 0) + diag * tk
                qc = lax.broadcasted_iota(I32, (tk, 2 * tq), 1)
                qc = jnp.where(qc >= tq, qc - tq, qc)
                s = jnp.where(kr <= qc, s, NEG_BIG)
            s_ref[hd, slot] = s
            cm_ref[hd, slot] = jnp.max(s, axis=0, keepdims=True)

    def update(key_block, slot):
        start = pl.multiple_of(key_block * tk, tk)
        for hd in heads:
            vtj = vt_ref[0, hd * DIFF_DV:(hd + 1) * DIFF_DV, pl.ds(start, tk)]
            m_old = m_ref[hd]
            m_new = jnp.maximum(m_old, cm_ref[hd, slot])
            alpha = jnp.exp2(m_old - m_new)
            p = jnp.exp2(s_ref[hd, slot] - m_new).astype(BF16)
            vt_ext = jnp.concatenate([vtj, ones_rows], axis=0)
            acc_ref[hd] = alpha * acc_ref[hd] + _dot(vt_ext, p)
            m_ref[hd] = m_new

    base = 2 * i
    scores(base, 0, 0)
    scores(base + 1, 1, 1)
    update(base, 0)

    def pair(jj, carry):
        prev = jnp.where(jj == 0, base + 1, 2 * jj - 1)
        scores(2 * jj, 0, None)
        update(prev, 1)
        scores(2 * jj + 1, 1, None)
        update(2 * jj, 0)
        return carry

    def two_pairs(jq, carry):
        pair(2 * jq, carry)
        pair(2 * jq + 1, carry)
        return carry

    lax.fori_loop(0, i // 2, two_pairs, 0)

    @pl.when(i % 2 == 1)
    def _():
        pair(i - 1, 0)

    update(jnp.where(i == 0, base + 1, base - 1), 1)

    for hd in heads:
        acc = acc_ref[hd]
        o = acc[:DIFF_DV] / acc[DIFF_DV:DIFF_DV + 1]
        od = o[:, :tq] - lam_ref[0] * o[:, tq:]
        od = od * lax.rsqrt(jnp.mean(od * od, axis=0, keepdims=True) + RMS_EPS) * nw_ref[...] * out_scale
        o_ref[0, :, hd * DIFF_DV:(hd + 1) * DIFF_DV] = od.T.astype(o_ref.dtype)


def _diff_attn(lam, qt, k3, vt, nw_col, tq, out_scale):
    b, s, _ = k3.shape
    nh = DIFF_HEADS
    hp = DIFF_HEADS_PER_STEP
    tk = tq // 2
    wq, wv = hp * 2 * DIFF_DQK, hp * DIFF_DV
    return pl.pallas_call(
        functools.partial(_diff_kernel, tq=tq, tk=tk, out_scale=out_scale),
        grid=(b, nh // hp, s // tq),
        in_specs=[pl.BlockSpec(memory_space=pltpu.SMEM),
                  pl.BlockSpec((1, wq, tq), lambda bb, h, i: (bb, h, i)),
                  pl.BlockSpec((1, s, wq), lambda bb, h, i: (bb, 0, h)),
                  pl.BlockSpec((1, wv, s), lambda bb, h, i: (bb, h, 0)),
                  pl.BlockSpec(nw_col.shape, lambda bb, h, i: (0, 0))],
        out_specs=pl.BlockSpec((1, tq, wv), lambda bb, h, i: (bb, i, h)),
        out_shape=jax.ShapeDtypeStruct((b, s, nh * DIFF_DV), BF16),
        scratch_shapes=[pltpu.VMEM((hp, 1, 2 * tq), F32),
                        pltpu.VMEM((hp, DIFF_DV + BF16_SUBLANES, 2 * tq), F32),
                        pltpu.VMEM((hp, 2, tk, 2 * tq), F32), pltpu.VMEM((hp, 2, 1, 2 * tq), F32)],
        compiler_params=pltpu.CompilerParams(dimension_semantics=("parallel", "parallel", "arbitrary"),
                                             vmem_limit_bytes=VMEM_LIMIT),
        name="diff_attn",
    )(lam, qt, k3, vt, nw_col)


def _memkv_kernel(mem_ref, wk_ref, wv_ref, kt_ref, v_ref, *, scale):
    mb = mem_ref[0].astype(BF16)
    k = _dot(mb, wk_ref[...])
    kt_ref[0] = (k.T * scale).astype(BF16)
    v_ref[0] = _dot(mb, wv_ref[...]).astype(BF16)


def _memkv(mem, wk, wv, scale):
    b, m, d = mem.shape
    return pl.pallas_call(
        functools.partial(_memkv_kernel, scale=scale),
        grid=(b,),
        in_specs=[pl.BlockSpec((1, m, d), lambda bb: (bb, 0, 0)),
                  pl.BlockSpec(wk.shape, lambda bb: (0, 0)),
                  pl.BlockSpec(wv.shape, lambda bb: (0, 0))],
        out_specs=[pl.BlockSpec((1, d, m), lambda bb: (bb, 0, 0)),
                   pl.BlockSpec((1, m, d), lambda bb: (bb, 0, 0))],
        out_shape=[jax.ShapeDtypeStruct((b, d, m), BF16), jax.ShapeDtypeStruct((b, m, d), BF16)],
        compiler_params=pltpu.CompilerParams(dimension_semantics=("parallel",),
                                             vmem_limit_bytes=VMEM_LIMIT),
        name="memkv",
    )(mem, wk, wv)


def _mid_kernel(x_ref, og_ref, od_ref, wo1_ref, wo2_ref, g1_ref, b1_ref, wq_ref, kt_ref, v_ref, wo_ref,
                g2_ref, b2_ref, rw_ref, rw1_ref, rb_ref, x2_ref, x2p_ref, route_ref, gate_ref, cnt_ref, run_ref):
    i = pl.program_id(0)

    @pl.when(i == 0)
    def _():
        run_ref[...] = jnp.zeros_like(run_ref)

    tm, d = x_ref.shape
    sub = tm // MID_SUBBLOCKS
    spans = [(k * sub, (k + 1) * sub) for k in range(MID_SUBBLOCKS)]
    dh = d // XA_HEADS

    h = [_dot(og_ref[a:b, :], wo1_ref[...]) + _dot(od_ref[a:b, :], wo2_ref[...]) for a, b in spans]
    x1 = [_layer_norm(DEEPNORM_ALPHA * x_ref[a:b, :] + hk, g1_ref[...], b1_ref[...])
          for (a, b), hk in zip(spans, h)]
    q = [_dot(xk.astype(BF16), wq_ref[...]).astype(BF16) for xk in x1]
    heads = [[] for _ in spans]
    for hh in range(XA_HEADS):
        lo, hi = hh * dh, (hh + 1) * dh
        s = [_dot(qk[:, lo:hi], kt_ref[0, lo:hi, :]) for qk in q]
        p = [jnp.exp(sk - jnp.max(sk, axis=-1, keepdims=True)) for sk in s]
        p = [(pk / jnp.sum(pk, axis=-1, keepdims=True)).astype(BF16) for pk in p]
        for k, pk in enumerate(p):
            heads[k].append(_dot(pk, v_ref[0, :, lo:hi]))
    o = [jnp.concatenate(hk, axis=-1).astype(BF16) for hk in heads]
    h2 = [_dot(ok, wo_ref[...]) for ok in o]
    x2 = [_layer_norm(DEEPNORM_ALPHA * xk + hk, g2_ref[...], b2_ref[...]) for xk, hk in zip(x1, h2)]

    x_hi = [xk.astype(BF16) for xk in x2]
    x_lo = [(xk - hk.astype(F32)).astype(BF16) for xk, hk in zip(x2, x_hi)]
    r1 = [_dot(hk, rw_ref[...]) for hk in x_hi]
    r2 = [_dot(lk, rw1_ref[...]) for lk in x_lo]
    logits = [a1 + pltpu.roll(a1, LANES - N_EXPERTS, 1) + a2 + rb_ref[...] for a1, a2 in zip(r1, r2)]

    lane = lax.broadcasted_iota(I32, (sub, LANES), 1)
    rr = lax.broadcasted_iota(I32, (sub, sub), 0)
    cc = lax.broadcasted_iota(I32, (sub, sub), 1)
    tri = jnp.where(cc < rr, 1.0, 0.0).astype(BF16)
    run = run_ref[...]
    for (a, b), xk, lgk in zip(spans, x2, logits):
        x2_ref[a:b, :] = xk
        packed = _pack_bf16_pairs(xk)
        for piece in range(x2p_ref.shape[0]):
            x2p_ref[piece, a:b, :] = packed[:, piece * SC_ROW_WORDS:(piece + 1) * SC_ROW_WORDS]
        lg = jnp.where(lane < N_EXPERTS, lgk, NEG_BIG)
        sel = jnp.zeros((sub, LANES), F32)
        vals, hots, idxs = [], [], []
        for _ in range(TOP_K):
            mx = jnp.max(lg, axis=-1, keepdims=True)
            idx = jnp.min(jnp.where(lg == mx, lane, LANES), axis=-1, keepdims=True)
            hot = lane == idx
            vals.append(mx)
            idxs.append(idx)
            hots.append(hot)
            lg = jnp.where(hot, NEG_BIG, lg)
            sel = sel + jnp.where(hot, 1.0, 0.0)
        ex = [jnp.exp(v - vals[0]) for v in vals]
        den = ex[0] + ex[1] + ex[2] + ex[3]
        before = _dot(tri, sel.astype(BF16)) + run
        route = jnp.zeros((sub, LANES), F32)
        gates = jnp.zeros((sub, LANES), F32)
        for kk in range(TOP_K):
            rank = jnp.sum(jnp.where(hots[kk], before, 0.0), axis=-1, keepdims=True)
            route = jnp.where(lane == kk, idxs[kk].astype(F32), route)
            route = jnp.where(lane == TOP_K + kk, rank, route)
            gates = jnp.where(lane == kk, ex[kk] / den, gates)
        route_ref[:, a:b] = route.T[:2 * TOP_K].astype(I32)
        gate_ref[a:b, :] = gates
        run = run + jnp.sum(sel, axis=0, keepdims=True)
    run_ref[...] = run
    cnt_ref[...] = run.astype(I32)


def _mid(x2d, og, od, wo1, wo2, g1, b1, wq, kt, v, wo, g2, b2, rw, rw1, rb, tm, rows_per_batch):
    t, d = x2d.shape
    blocks_per_batch = rows_per_batch // tm
    pieces = (d // 2) // SC_ROW_WORDS
    row = lambda w: pl.BlockSpec((tm, w), lambda i: (i, 0))
    full = lambda a: pl.BlockSpec(a.shape, lambda i: (0, 0))
    per_batch = lambda a: pl.BlockSpec((1,) + a.shape[1:], lambda i: (i // blocks_per_batch, 0, 0))
    return pl.pallas_call(
        _mid_kernel,
        grid=(t // tm,),
        in_specs=[row(d), row(og.shape[1]), row(od.shape[1]), full(wo1), full(wo2), full(g1), full(b1),
                  full(wq), per_batch(kt), per_batch(v), full(wo), full(g2), full(b2), full(rw), full(rw1), full(rb)],
        out_specs=[row(d), pl.BlockSpec((pieces, tm, SC_ROW_WORDS), lambda i: (0, i, 0)),
                   pl.BlockSpec((2 * TOP_K, tm), lambda i: (0, i)), row(LANES),
                   pl.BlockSpec((1, LANES), lambda i: (0, 0))],
        out_shape=[jax.ShapeDtypeStruct((t, d), F32), jax.ShapeDtypeStruct((pieces, t, SC_ROW_WORDS), U32),
                   jax.ShapeDtypeStruct((2 * TOP_K, t), I32), jax.ShapeDtypeStruct((t, LANES), F32),
                   jax.ShapeDtypeStruct((1, LANES), I32)],
        scratch_shapes=[pltpu.VMEM((1, LANES), F32)],
        compiler_params=pltpu.CompilerParams(dimension_semantics=("arbitrary",),
                                             vmem_limit_bytes=VMEM_LIMIT),
        name="mid",
    )(x2d, og, od, wo1, wo2, g1, b1, wq, kt, v, wo, g2, b2, rw, rw1, rb)


def _sc_scatter_rows(rows, idx4, n_out):
    t, d = rows.shape
    mesh = plsc.VectorSubcoreMesh(core_axis_name="c", subcore_axis_name="s")

    @functools.partial(pl.kernel, out_type=jax.ShapeDtypeStruct((n_out, d), rows.dtype), mesh=mesh)
    def scatter_kernel(x_hbm, i0_hbm, i1_hbm, i2_hbm, i3_hbm, o_hbm):
        def body(x_vmem, i0, i1, i2, i3):
            for iv in (i0, i1, i2, i3):
                pltpu.sync_copy(x_vmem, o_hbm.at[iv.at[0]])

        ispec = pl.BlockSpec((1, SC_WINDOW), lambda i: (0, i))
        pltpu.emit_pipeline(
            body,
            grid=(t // SC_WINDOW,),
            in_specs=[pl.BlockSpec((SC_WINDOW, d), lambda i: (i, 0)), ispec, ispec, ispec, ispec],
            out_specs=[],
            core_axis_name=("c", "s"),
            dimension_semantics=(pltpu.PARALLEL,),
        )(x_hbm, i0_hbm, i1_hbm, i2_hbm, i3_hbm)

    return scatter_kernel(rows, *[idx4[k:k + 1] for k in range(TOP_K)])


def _sc_gather_rows(table, idx):
    n = idx.shape[0]
    d = table.shape[1]
    mesh = plsc.VectorSubcoreMesh(core_axis_name="c", subcore_axis_name="s")

    @functools.partial(pl.kernel, out_type=jax.ShapeDtypeStruct((n, d), table.dtype), mesh=mesh)
    def gather_kernel(x_hbm, i_hbm, o_hbm):
        def body(i_vmem, o_vmem):
            pltpu.sync_copy(x_hbm.at[i_vmem.at[0]], o_vmem)

        pltpu.emit_pipeline(
            body,
            grid=(n // SC_WINDOW,),
            in_specs=[pl.BlockSpec((1, SC_WINDOW), lambda i: (0, i))],
            out_specs=[pl.BlockSpec((SC_WINDOW, d), lambda i: (i, 0))],
            core_axis_name=("c", "s"),
            dimension_semantics=(pltpu.PARALLEL,),
        )(i_hbm, o_hbm)

    return gather_kernel(table, idx.reshape(1, n))


def _moe_kernel(be_ref, nu_ref, nx_ref, x_ref, wg_hbm, bg_ref, wu_hbm, bu_ref, wd_hbm, bd_ref, y_ref,
                stage_ref, wgb_ref, wub_ref, wdb_ref, sem):
    i = pl.program_id(0)
    used = i < nu_ref[0]
    expert = be_ref[i]
    prev = be_ref[jnp.maximum(i - 1, 0)]
    changed = jnp.logical_and(used, jnp.logical_or(i == 0, expert != prev))

    def weight_copies(e):
        return [pltpu.make_async_copy(w.at[e], stage_ref.at[k], sem.at[k])
                for k, w in enumerate((wg_hbm, wu_hbm, wd_hbm))]

    @pl.when(changed)
    def _():
        @pl.when(i == 0)
        def _():
            for cp in weight_copies(expert):
                cp.start()

        for cp in weight_copies(expert):
            cp.wait()
        wgb_ref[...] = stage_ref[0].astype(BF16)
        wub_ref[...] = stage_ref[1].astype(BF16)
        wdb_ref[...] = stage_ref[2].astype(BF16)
        nxt = nx_ref[i]

        @pl.when(nxt >= 0)
        def _():
            for cp in weight_copies(nxt):
                cp.start()

    @pl.when(used)
    def _():
        halves = [_unpack_bf16_pairs(x_ref[piece]) for piece in range(x_ref.shape[0])]
        x = jnp.concatenate([h[0] for h in halves] + [h[1] for h in halves], axis=-1).astype(BF16)
        hg = _dot(x, wgb_ref[...]) + bg_ref[0]
        hl = _dot(x, wub_ref[...]) + bu_ref[0]
        hg = jnp.minimum(hg, SWIGLU_LIMIT)
        hl = jnp.clip(hl, -SWIGLU_LIMIT, SWIGLU_LIMIT)
        act = hg * _sigmoid(SWIGLU_ALPHA * hg) * (hl + 1.0)
        y = _dot(act.astype(BF16), wdb_ref[...]) + bd_ref[0]
        packed = _pack_bf16_pairs(y)
        for piece in range(y_ref.shape[0]):
            y_ref[piece] = packed[:, piece * SC_ROW_WORDS:(piece + 1) * SC_ROW_WORDS]

    @pl.when(jnp.logical_not(used))
    def _():
        y_ref[...] = jnp.zeros_like(y_ref)


def _moe(block_e, n_used, next_e, xs, wg, bg, wu, bu, wd, bd, tm):
    pieces, p, dp = xs.shape
    d, dff = wg.shape[1], wg.shape[2]
    assert d == dff, "the shared f32 staging buffer assumes square expert matrices"
    n_blocks = p // tm
    bspec = lambda a: pl.BlockSpec((1,) + a.shape[1:], lambda i, be, nu, nx: (be[i], 0, 0))
    hbm = pl.BlockSpec(memory_space=pl.ANY)
    rows = pl.BlockSpec((pieces, tm, dp), lambda i, be, nu, nx: (0, i, 0))
    grid_spec = pltpu.PrefetchScalarGridSpec(
        num_scalar_prefetch=3,
        grid=(n_blocks,),
        in_specs=[rows, hbm, bspec(bg), hbm, bspec(bu), hbm, bspec(bd)],
        out_specs=rows,
        scratch_shapes=[pltpu.VMEM((3, d, dff), F32),
                        pltpu.VMEM((d, dff), BF16), pltpu.VMEM((d, dff), BF16), pltpu.VMEM((dff, d), BF16),
                        pltpu.SemaphoreType.DMA((3,))],
    )
    return pl.pallas_call(
        _moe_kernel,
        grid_spec=grid_spec,
        out_shape=jax.ShapeDtypeStruct((pieces, p, dp), U32),
        compiler_params=pltpu.CompilerParams(dimension_semantics=("arbitrary",),
                                             vmem_limit_bytes=VMEM_LIMIT),
        name="moe_ffn",
    )(block_e, n_used, next_e, xs, wg, bg, wu, bu, wd, bd)


def _final_kernel(x_ref, y4_ref, gate_ref, g_ref, b_ref, o_ref):
    pieces = y4_ref.shape[0]
    gates = gate_ref[...]
    lows = [0.0] * pieces
    highs = [0.0] * pieces
    for kk in range(TOP_K):
        gk = gates[:, kk:kk + 1]
        for piece in range(pieces):
            lo, hi = _unpack_bf16_pairs(y4_ref[piece, kk])
            lows[piece] = lows[piece] + gk * lo
            highs[piece] = highs[piece] + gk * hi
    y = jnp.concatenate(lows + highs, axis=-1)
    o_ref[...] = _layer_norm(DEEPNORM_ALPHA * x_ref[...] + y, g_ref[...], b_ref[...])


def _final(x2, y4, gates, g, b, tm):
    t, d = x2.shape
    row = lambda w: pl.BlockSpec((tm, w), lambda i: (i, 0))
    full = lambda a: pl.BlockSpec(a.shape, lambda i: (0, 0))
    return pl.pallas_call(
        _final_kernel,
        grid=(t // tm,),
        in_specs=[row(d), pl.BlockSpec(y4.shape[:2] + (tm, y4.shape[3]), lambda i: (0, 0, i, 0)),
                  row(gates.shape[1]), full(g), full(b)],
        out_specs=row(d),
        out_shape=jax.ShapeDtypeStruct((t, d), F32),
        compiler_params=pltpu.CompilerParams(dimension_semantics=("parallel",),
                                             vmem_limit_bytes=VMEM_LIMIT),
        name="final_ln",
    )(x2, y4, gates, g, b)


def _layout(route, counts, tm):
    t = route.shape[1]
    experts = route[:TOP_K]
    ranks = route[TOP_K:]
    counts = counts[0, :N_EXPERTS]
    padded = (counts + tm - 1) // tm * tm
    pad_end = jnp.cumsum(padded)
    pad_start = pad_end - padded
    eids = jnp.arange(N_EXPERTS, dtype=I32)[:, None, None]
    dest = ranks + jnp.sum(jnp.where(experts[None] == eids, pad_start[:, None, None], 0), axis=0)
    n_blocks = t * TOP_K // tm + N_EXPERTS
    starts = jnp.arange(n_blocks, dtype=I32) * tm
    block_e = jnp.minimum(jnp.sum((pad_end[None, :] <= starts[:, None]).astype(I32), axis=-1), N_EXPERTS - 1)
    n_used = (pad_end[-1] // tm).astype(I32).reshape(1)
    e_row = jnp.arange(N_EXPERTS, dtype=I32)
    later = (e_row[None, :] > e_row[:, None]) & (padded[None, :] > 0)
    next_of = jnp.min(jnp.where(later, e_row[None, :], N_EXPERTS), axis=1)
    next_of = jnp.where(next_of == N_EXPERTS, -1, next_of)
    next_e = jnp.sum(jnp.where(block_e[:, None] == e_row[None, :], next_of[None, :], 0), axis=1)
    return dest.astype(I32), block_e.astype(I32), n_used, next_e.astype(I32), n_blocks


def kernel(x, mem, w_in, gdn_conv_w, gdn_a_log, gdn_dt_bias, gdn_norm_w, diff_lq1, diff_lk1, diff_lq2, diff_lk2,
           diff_norm_w, w_out, ln1_g, ln1_b, xa_wq, xa_wk, xa_wv, xa_wo, ln2_g, ln2_b, router_w, router_b,
           exp_w_gate, exp_b_gate, exp_w_up, exp_b_up, exp_w_down, exp_b_down, ln3_g, ln3_b):
    b, s, d = x.shape
    t = b * s
    x2d = x.reshape(t, d)
    for l in range(DEPTH):
        lambda_init = 0.8 - 0.6 * math.exp(-0.3 * l)
        n_g = 4 * GDN_W
        w = w_in[l]
        wg = w[:, :n_g].astype(BF16)
        small = w[:, n_g:n_g + 2 * GDN_HEADS]
        ws = jnp.pad(small, ((0, 0), (0, SMALL_W - 2 * GDN_HEADS))).astype(BF16)
        d0 = n_g + 2 * GDN_HEADS
        wdq = w[:, d0:d0 + DIFF_W].astype(BF16)
        wdk = w[:, d0 + DIFF_W:d0 + 2 * DIFF_W].astype(BF16)
        wdv = w[:, d0 + 2 * DIFF_W:d0 + 3 * DIFF_W].astype(BF16)
        conv_w = gdn_conv_w[l].reshape(GDN_CONV, 3 * GDN_W).astype(F32)
        ab = jnp.zeros((2, SMALL_W), F32)
        ab = ab.at[0, GDN_HEADS:2 * GDN_HEADS].set(-jnp.exp(gdn_a_log[l].astype(F32)))
        ab = ab.at[1, GDN_HEADS:2 * GDN_HEADS].set(gdn_dt_bias[l].astype(F32))
        lam = (jnp.exp(jnp.sum(diff_lq1[l].astype(F32) * diff_lk1[l].astype(F32)))
               - jnp.exp(jnp.sum(diff_lq2[l].astype(F32) * diff_lk2[l].astype(F32))) + lambda_init).reshape(1)

        g2d, dqt, dk2d, dvt, s2d = _inproj(x2d, wg, wdq, wdk, wdv, ws, conv_w, tm=512, batch=b,
                                           q_scale=LOG2E * DIFF_DQK ** -0.5)
        o_gdn = _gdn(g2d.reshape(b, s, n_g), s2d.reshape(b, s, SMALL_W), ab,
                     gdn_norm_w[l].reshape(1, GDN_DV).astype(F32), ts=256)
        o_diff = _diff_attn(lam, dqt, dk2d.reshape(b, s, DIFF_W), dvt,
                            diff_norm_w[l].reshape(DIFF_DV, 1).astype(F32), tq=512, out_scale=1.0 - lambda_init)

        dh = d // XA_HEADS
        kt, v = _memkv(mem, xa_wk[l].astype(BF16), xa_wv[l].astype(BF16), scale=dh ** -0.5)
        wo_mix = w_out[l].astype(BF16)
        rw_f = router_w[l].astype(F32)
        rw_hi = rw_f.astype(BF16)
        rw_lo = (rw_f - rw_hi.astype(F32)).astype(BF16)
        rw = jnp.pad(jnp.concatenate([rw_hi, rw_lo], axis=1), ((0, 0), (0, LANES - 2 * N_EXPERTS)))
        rw1 = jnp.pad(rw_hi, ((0, 0), (0, LANES - N_EXPERTS)))
        rb = jnp.pad(router_b[l].astype(F32), (0, LANES - N_EXPERTS)).reshape(1, LANES)
        row = lambda a: a.reshape(1, -1).astype(F32)
        x2, x2p, route, gates, counts = _mid(
            x2d, o_gdn.reshape(t, GDN_W), o_diff.reshape(t, DIFF_HEADS * DIFF_DV),
            wo_mix[:GDN_W], wo_mix[GDN_W:], row(ln1_g[l]), row(ln1_b[l]),
            xa_wq[l].astype(BF16), kt, v, xa_wo[l].astype(BF16), row(ln2_g[l]), row(ln2_b[l]),
            rw, rw1, rb, tm=1024, rows_per_batch=s)

        tm_moe = 256
        dest, block_e, n_used, next_e, n_blocks = _layout(route, counts, tm_moe)
        pieces = x2p.shape[0]
        p_rows = n_blocks * tm_moe
        offs = (jnp.arange(pieces, dtype=I32) * p_rows)[:, None]
        idx_scatter = (dest[:, None, :] + offs[None]).reshape(TOP_K, pieces * t)
        xs = _sc_scatter_rows(x2p.reshape(pieces * t, SC_ROW_WORDS), idx_scatter,
                              pieces * p_rows).reshape(pieces, p_rows, SC_ROW_WORDS)
        ys = _moe(block_e, n_used, next_e, xs,
                  exp_w_gate[l], exp_b_gate[l].reshape(N_EXPERTS, 1, -1), exp_w_up[l],
                  exp_b_up[l].reshape(N_EXPERTS, 1, -1), exp_w_down[l], exp_b_down[l].reshape(N_EXPERTS, 1, -1),
                  tm=tm_moe)
        idx_gather = (dest.reshape(1, TOP_K * t) + offs).reshape(pieces * TOP_K * t)
        y4 = _sc_gather_rows(ys.reshape(pieces * p_rows, SC_ROW_WORDS),
                             idx_gather).reshape(pieces, TOP_K, t, SC_ROW_WORDS)
        x2d = _final(x2, y4, gates, row(ln3_g[l]), row(ln3_b[l]), tm=512)
    return x2d.reshape(b, s, d)
```

```python
import functools
import math

import jax
import jax.numpy as jnp
from jax import lax
from jax.experimental import pallas as pl
from jax.experimental.pallas import tpu as pltpu
from jax.experimental.pallas import tpu_sc as plsc

F32 = jnp.float32
BF16 = jnp.bfloat16
U32 = jnp.uint32
I32 = jnp.int32
HIGHEST = lax.Precision.HIGHEST

GDN_HEADS = 4
GDN_DK = 128
GDN_DV = 128
GDN_CONV = 4
GDN_CHUNK = 64
DIFF_HEADS = 4
DIFF_DQK = 64
DIFF_DV = 128
XA_HEADS = 4
N_EXPERTS = 32
TOP_K = 4
SWIGLU_ALPHA = 1.702
SWIGLU_LIMIT = 7.0
LN_EPS = 1e-5
RMS_EPS = 1e-6
DEPTH = 1
DEEPNORM_ALPHA = (2 * DEPTH) ** 0.25

GDN_W = GDN_HEADS * GDN_DK
DIFF_W = DIFF_HEADS * 2 * DIFF_DQK
SMALL_W = 128

LANES = 128
SUBLANES = 8
BF16_SUBLANES = 16
VMEM_LIMIT = 56 * 1024 * 1024
SC_WINDOW = 128
SC_ROW_WORDS = 256
DIFF_HEADS_PER_STEP = 2
DIFF_LANE_STRIP = 256
MID_SUBBLOCKS = 4

NEG_BIG = -1e30
LOG2E = 1.4426950408889634


def _dot(a, b):
    return jnp.dot(a, b, preferred_element_type=F32)


def _dot_nt(a, b):
    return lax.dot_general(a, b, (((1,), (1,)), ((), ())), preferred_element_type=F32)


def _dot_tn(a, b):
    return lax.dot_general(a, b, (((0,), (0,)), ((), ())), preferred_element_type=F32)


def _layer_norm(y, g, b):
    mu = jnp.mean(y, axis=-1, keepdims=True)
    d = y - mu
    var = jnp.mean(d * d, axis=-1, keepdims=True)
    return d * lax.rsqrt(var + LN_EPS) * g + b


def _sigmoid(x):
    return 1.0 / (1.0 + jnp.exp(-x))


def _pack_bf16_pairs(x):
    n = x.shape[1] // 2
    bits = pltpu.bitcast(x.astype(BF16).astype(F32), U32)
    return (bits[:, :n] >> 16) | bits[:, n:]


def _unpack_bf16_pairs(w):
    lo = pltpu.bitcast(w << 16, F32)
    hi = pltpu.bitcast(w & jnp.uint32(0xFFFF0000), F32)
    return lo, hi


def _inproj_kernel(x_ref, wg_ref, wq_ref, wk_ref, wv_ref, ws_ref, cw_ref, g_ref, qt_ref, k_ref, vt_ref, s_ref,
                   tail_ref, *, q_scale, per_batch):
    i = pl.program_id(0)
    tm = x_ref.shape[0]
    n_conv = 3 * GDN_W

    @pl.when(i == 0)
    def _():
        tail_ref[...] = jnp.zeros_like(tail_ref)

    xb = x_ref[...].astype(BF16)
    first = i % per_batch == 0
    slab = GDN_W
    row8 = lax.broadcasted_iota(I32, (SUBLANES, slab), 0)

    def proj_z_qt():
        g_ref[:, n_conv:] = _dot(xb, wg_ref[:, n_conv:]).astype(BF16)
        qt_ref[0] = (_dot(xb, wq_ref[...]) * q_scale).T.astype(BF16)

    def proj_k():
        k_ref[...] = _dot(xb, wk_ref[...]).astype(BF16)

    def proj_vt_s():
        vt_ref[0] = _dot(xb, wv_ref[...]).T.astype(BF16)
        s_ref[...] = _dot(xb, ws_ref[...])

    others = [proj_z_qt, proj_k, proj_vt_s]
    assert n_conv // slab == len(others)

    for c0 in range(0, n_conv, slab):
        pre = _dot(xb, wg_ref[:, c0:c0 + slab])
        others[c0 // slab]()
        halo = jnp.where(first, 0.0, tail_ref[:, c0:c0 + slab])
        tail_ref[:, c0:c0 + slab] = pre[tm - SUBLANES:, :]
        w = cw_ref[:, c0:c0 + slab]
        y = pre * w[GDN_CONV - 1:GDN_CONV, :]
        for j in range(1, GDN_CONV):
            xr = pltpu.roll(pre, j, 0)
            top = jnp.where(row8 < j, pltpu.roll(halo, j, 0), xr[0:SUBLANES])
            xr = jnp.concatenate([top, xr[SUBLANES:]], axis=0)
            y = y + xr * w[GDN_CONV - 1 - j:GDN_CONV - j, :]
        y = y * _sigmoid(y)
        for lo in range(0, slab, GDN_DK):
            yh = y[:, lo:lo + GDN_DK]
            if c0 < GDN_W:
                yh = yh * (lax.rsqrt(jnp.sum(yh * yh, axis=-1, keepdims=True) + RMS_EPS) * (GDN_DK ** -0.5))
            elif c0 < 2 * GDN_W:
                yh = yh * lax.rsqrt(jnp.sum(yh * yh, axis=-1, keepdims=True) + RMS_EPS)
            g_ref[:, c0 + lo:c0 + lo + GDN_DK] = yh.astype(BF16)


def _inproj(x2d, wg, wq, wk, wv, ws, conv_w, tm, batch, q_scale):
    t, d = x2d.shape
    s = t // batch
    per_batch = s // tm
    full = lambda a: pl.BlockSpec(a.shape, lambda i: (0, 0))
    row = lambda w: pl.BlockSpec((tm, w), lambda i: (i, 0))
    tr = lambda w: pl.BlockSpec((1, w, tm), lambda i: (i // per_batch, 0, i % per_batch))
    return pl.pallas_call(
        functools.partial(_inproj_kernel, q_scale=q_scale, per_batch=per_batch),
        grid=(t // tm,),
        in_specs=[row(d), full(wg), full(wq), full(wk), full(wv), full(ws), full(conv_w)],
        out_specs=[row(wg.shape[1]), tr(wq.shape[1]), row(wk.shape[1]), tr(wv.shape[1]), row(ws.shape[1])],
        out_shape=[jax.ShapeDtypeStruct((t, wg.shape[1]), BF16),
                   jax.ShapeDtypeStruct((batch, wq.shape[1], s), BF16),
                   jax.ShapeDtypeStruct((t, wk.shape[1]), BF16),
                   jax.ShapeDtypeStruct((batch, wv.shape[1], s), BF16),
                   jax.ShapeDtypeStruct((t, ws.shape[1]), F32)],
        scratch_shapes=[pltpu.VMEM((SUBLANES, conv_w.shape[1]), F32)],
        compiler_params=pltpu.CompilerParams(dimension_semantics=("arbitrary",),
                                             vmem_limit_bytes=VMEM_LIMIT),
        name="inproj",
    )(x2d, wg, wq, wk, wv, ws, conv_w)


def _gdn_kernel(q_ref, k_ref, v_ref, z_ref, s_ref, ab_ref, nw_ref, o_ref,
                state_ref, us_ref, wq_ref, ai_ref, kt_ref, gl_ref, *, ts):
    c_len = GDN_CHUNK
    n_chunks = ts // c_len
    nh = GDN_HEADS
    n = nh * n_chunks
    i = pl.program_id(1)
    w_slot = i % 2
    r_slot = 1 - w_slot

    @pl.when(i == 0)
    def _():
        state_ref[...] = jnp.zeros_like(state_ref)
        us_ref[...] = jnp.zeros_like(us_ref)
        wq_ref[...] = jnp.zeros_like(wq_ref)
        ai_ref[...] = jnp.zeros_like(ai_ref)
        kt_ref[...] = jnp.zeros_like(kt_ref)
        gl_ref[...] = jnp.zeros_like(gl_ref)

    nw = nw_ref[...]
    rec = {"states": [state_ref[h] for h in range(nh)], "outs": [[None] * n_chunks for _ in range(nh)]}
    r_wq = [wq_ref[r_slot, b] for b in range(n)]
    r_us = [us_ref[r_slot, b] for b in range(n)]
    r_ai = [ai_ref[r_slot, b] for b in range(n)]
    r_kt = [kt_ref[r_slot, b] for b in range(n)]
    r_gl = [gl_ref[r_slot, b] for b in range(n)]

    def rec_read_state(cc):
        idx = [h * n_chunks + cc for h in range(nh)]
        sb = [st.astype(BF16) for st in rec["states"]]
        ws_qs = [_dot(r_wq[idx[h]], sb[h]) for h in range(nh)]
        v_new = [r_us[idx[h]] - ws_qs[h][:c_len] for h in range(nh)]
        rec["ws_qs"] = ws_qs
        rec["vnb"] = [v.astype(BF16) for v in v_new]

    def rec_write_state(cc):
        idx = [h * n_chunks + cc for h in range(nh)]
        for h in range(nh):
            rec["outs"][h][cc] = rec["ws_qs"][h][c_len:] + _dot(r_ai[idx[h]], rec["vnb"][h])
        rec["states"] = [rec["states"][h] * r_gl[idx[h]] + _dot_tn(r_kt[idx[h]], rec["vnb"][h])
                         for h in range(nh)]

    def rec_finish():
        for h in range(nh):
            lo, hi = h * GDN_DK, (h + 1) * GDN_DK
            state_ref[h] = rec["states"][h]
            o = jnp.concatenate(rec["outs"][h], axis=0)
            o = o * lax.rsqrt(jnp.mean(o * o, axis=-1, keepdims=True) + RMS_EPS) * nw
            z = z_ref[0, :, lo:hi].astype(F32)
            o_ref[0, :, lo:hi] = (o * (z * _sigmoid(z))).astype(o_ref.dtype)

    rec_stages = []
    for cc in range(n_chunks):
        rec_stages += [functools.partial(rec_read_state, cc), functools.partial(rec_write_state, cc)]
    rec_stages.append(rec_finish)

    def weave():
        if rec_stages:
            rec_stages.pop(0)()

    qa = q_ref[0].astype(F32)
    ka = k_ref[0].astype(F32)
    va = v_ref[0].astype(F32)

    sg = s_ref[0]
    beta_all = _sigmoid(sg)
    sp_in = sg + ab_ref[1:2, :]
    softplus = jnp.maximum(sp_in, 0.0) + jnp.log(1.0 + jnp.exp(-jnp.abs(sp_in)))
    g_step = ab_ref[0:1, :] * softplus

    r = lax.broadcasted_iota(I32, (ts, ts), 0)
    c = lax.broadcasted_iota(I32, (ts, ts), 1)
    tri = jnp.where((r // c_len) == (c // c_len), jnp.where(c <= r, 1.0, 0.0), 0.0)
    gc = jnp.dot(tri, g_step, precision=HIGHEST, preferred_element_type=F32)
    gct = [gc[cc * c_len:(cc + 1) * c_len, :].T for cc in range(n_chunks)]

    ri = lax.broadcasted_iota(I32, (c_len, c_len), 0)
    ci = lax.broadcasted_iota(I32, (c_len, c_len), 1)
    causal = ci <= ri
    strict = ci < ri

    qs, ks, kbs, vbs, kbes, decays, qds = [], [], [], [], [], [], []
    for h in range(nh):
        lo, hi = h * GDN_DK, (h + 1) * GDN_DK
        qh = qa[:, lo:hi]
        kh = ka[:, lo:hi]
        vh = va[:, lo:hi]
        beta = beta_all[:, h:h + 1]
        gcol_all = gc[:, nh + h:nh + h + 1]
        for cc in range(n_chunks):
            b = h * n_chunks + cc
            r0, r1 = cc * c_len, (cc + 1) * c_len
            qc, kc, vc = qh[r0:r1], kh[r0:r1], vh[r0:r1]
            bcol = beta[r0:r1]
            gcol = gcol_all[r0:r1]
            grow = gct[cc][nh + h:nh + h + 1, :]
            decay = jnp.where(causal, jnp.exp(jnp.where(causal, gcol - grow, 0.0)), 0.0)
            eg = jnp.exp(gcol)
            g_last = gcol[c_len - 1:c_len, :]
            kb = kc * bcol
            qs.append(qc.astype(BF16))
            ks.append(kc.astype(BF16))
            kbs.append(kb)
            vbs.append(vc * bcol)
            kbes.append(kb * eg)
            decays.append(decay)
            qds.append((qc * eg).astype(BF16))
            kt_ref[w_slot, b] = (kc * jnp.exp(g_last - gcol)).astype(BF16)
            gl_ref[w_slot, b] = jnp.broadcast_to(jnp.exp(g_last), (1, GDN_DV))
        weave()

    kq = [_dot_nt(jnp.concatenate([kbs[b].astype(BF16), qs[b]], axis=0), ks[b]) for b in range(n)]
    ms = [-jnp.where(strict, kq[b][:c_len] * decays[b], 0.0) for b in range(n)]
    for b in range(n):
        ai_ref[w_slot, b] = jnp.where(causal, kq[b][c_len:] * decays[b], 0.0).astype(BF16)
    weave()
    ys = ms
    for _ in range(5):
        mb = [m.astype(BF16) for m in ms]
        ms = [_dot(mb[b], mb[b]) for b in range(n)]
        weave()
        mb = [m.astype(BF16) for m in ms]
        ys = [ys[b] + ms[b] + _dot(ys[b].astype(BF16), mb[b]) for b in range(n)]
        weave()
    rhs = [jnp.concatenate([vbs[b], kbes[b]], axis=1) for b in range(n)]
    for b in range(n):
        uw = rhs[b] + _dot(ys[b].astype(BF16), rhs[b].astype(BF16))
        us_ref[w_slot, b] = uw[:, :GDN_DV]
        wq_ref[w_slot, b] = jnp.concatenate([uw[:, GDN_DV:].astype(BF16), qds[b]], axis=0)
    while rec_stages:
        weave()


def _gdn(g3, s3, ab, nw, ts):
    b, s, _ = g3.shape
    n_t = s // ts
    n = GDN_HEADS * (ts // GDN_CHUNK)
    cur = lambda i: jnp.minimum(i, n_t - 1)
    prev = lambda i: jnp.maximum(i - 1, 0)

    return pl.pallas_call(
        functools.partial(_gdn_kernel, ts=ts),
        grid=(b, n_t + 1),
        in_specs=[pl.BlockSpec((1, ts, GDN_W), lambda bb, i: (bb, cur(i), 0)),
                  pl.BlockSpec((1, ts, GDN_W), lambda bb, i: (bb, cur(i), 1)),
                  pl.BlockSpec((1, ts, GDN_W), lambda bb, i: (bb, cur(i), 2)),
                  pl.BlockSpec((1, ts, GDN_W), lambda bb, i: (bb, prev(i), 3)),
                  pl.BlockSpec((1, ts, SMALL_W), lambda bb, i: (bb, cur(i), 0)),
                  pl.BlockSpec(ab.shape, lambda bb, i: (0, 0)),
                  pl.BlockSpec(nw.shape, lambda bb, i: (0, 0))],
        out_specs=pl.BlockSpec((1, ts, GDN_W), lambda bb, i: (bb, prev(i), 0)),
        out_shape=jax.ShapeDtypeStruct((b, s, GDN_W), BF16),
        scratch_shapes=[pltpu.VMEM((GDN_HEADS, GDN_DK, GDN_DV), F32),
                        pltpu.VMEM((2, n, GDN_CHUNK, GDN_DV), F32),
                        pltpu.VMEM((2, n, 2 * GDN_CHUNK, GDN_DV), BF16),
                        pltpu.VMEM((2, n, GDN_CHUNK, GDN_CHUNK), BF16),
                        pltpu.VMEM((2, n, GDN_CHUNK, GDN_DK), BF16),
                        pltpu.VMEM((2, n, 1, GDN_DV), F32)],
        compiler_params=pltpu.CompilerParams(dimension_semantics=("parallel", "arbitrary"),
                                             vmem_limit_bytes=VMEM_LIMIT),
        name="gdn",
    )(g3, g3, g3, g3, s3, ab, nw)


def _diff_kernel(lam_ref, qt_ref, k_ref, vt_ref, nw_ref, o_ref, m_ref, acc_ref, s_ref, cm_ref, *, tq, tk,
                 out_scale):
    i = pl.program_id(2)
    heads = range(m_ref.shape[0])
    dq2 = 2 * DIFF_DQK
    rowi = lax.broadcasted_iota(I32, (dq2, tq), 0)
    qq = []
    for hd in heads:
        qt = qt_ref[0, hd * dq2:(hd + 1) * dq2, :]
        zero = jnp.zeros_like(qt)
        qq.append(jnp.concatenate([jnp.where(rowi < DIFF_DQK, qt, zero),
                                   jnp.where(rowi >= DIFF_DQK, qt, zero)], axis=1))
    m_ref[...] = jnp.full(m_ref.shape, NEG_BIG, F32)
    acc_ref[...] = jnp.zeros(acc_ref.shape, F32)
    ones_rows = jnp.ones((BF16_SUBLANES, tk), BF16)
    strip = DIFF_LANE_STRIP

    def scores(key_block, slot, diag):
        start = pl.multiple_of(key_block * tk, tk)
        for hd in heads:
            kj = k_ref[0, pl.ds(start, tk), hd * dq2:(hd + 1) * dq2]
            for c0 in range(0, 2 * tq, strip):
                s = _dot(kj, qq[hd][:, c0:c0 + strip])
                if diag is not None:
                    kr = lax.broadcasted_iota(I32, (tk, strip), 0) + diag * tk
                    qc = lax.broadcasted_iota(I32, (tk, strip), 1) + c0 % tq
                    s = jnp.where(kr <= qc, s, NEG_BIG)
                s_ref[hd, slot, :, c0:c0 + strip] = s
                cm_ref[hd, slot, :, c0:c0 + strip] = jnp.max(s, axis=0, keepdims=True)

    def update(key_block, slot):
        start = pl.multiple_of(key_block * tk, tk)
        for hd in heads:
            vtj = vt_ref[0, hd * DIFF_DV:(hd + 1) * DIFF_DV, pl.ds(start, tk)]
            vt_ext = jnp.concatenate([vtj, ones_rows], axis=0)
            for c0 in range(0, 2 * tq, strip):
                m_old = m_ref[hd, :, c0:c0 + strip]
                m_new = jnp.maximum(m_old, cm_ref[hd, slot, :, c0:c0 + strip])
                alpha = jnp.exp2(m_old - m_new)
                p = jnp.exp2(s_ref[hd, slot, :, c0:c0 + strip] - m_new).astype(BF16)
                acc_ref[hd, :, c0:c0 + strip] = alpha * acc_ref[hd, :, c0:c0 + strip] + _dot(vt_ext, p)
                m_ref[hd, :, c0:c0 + strip] = m_new

    base = 2 * i
    scores(base, 0, 0)
    scores(base + 1, 1, 1)
    update(base, 0)

    def pair(jj, carry):
        prev = jnp.where(jj == 0, base + 1, 2 * jj - 1)
        scores(2 * jj, 0, None)
        update(prev, 1)
        scores(2 * jj + 1, 1, None)
        update(2 * jj, 0)
        return carry

    def two_pairs(jq, carry):
        pair(2 * jq, carry)
        pair(2 * jq + 1, carry)
        return carry

    lax.fori_loop(0, i // 2, two_pairs, 0)

    @pl.when(i % 2 == 1)
    def _():
        pair(i - 1, 0)

    update(jnp.where(i == 0, base + 1, base - 1), 1)

    for hd in heads:
        acc = acc_ref[hd]
        o = acc[:DIFF_DV] / acc[DIFF_DV:DIFF_DV + 1]
        od = o[:, :tq] - lam_ref[0] * o[:, tq:]
        od = od * lax.rsqrt(jnp.mean(od * od, axis=0, keepdims=True) + RMS_EPS) * nw_ref[...] * out_scale
        o_ref[0, :, hd * DIFF_DV:(hd + 1) * DIFF_DV] = od.T.astype(o_ref.dtype)


def _diff_attn(lam, qt, k3, vt, nw_col, tq, out_scale):
    b, s, _ = k3.shape
    nh = DIFF_HEADS
    hp = DIFF_HEADS_PER_STEP
    tk = tq // 2
    wq, wv = hp * 2 * DIFF_DQK, hp * DIFF_DV
    return pl.pallas_call(
        functools.partial(_diff_kernel, tq=tq, tk=tk, out_scale=out_scale),
        grid=(b, nh // hp, s // tq),
        in_specs=[pl.BlockSpec(memory_space=pltpu.SMEM),
                  pl.BlockSpec((1, wq, tq), lambda bb, h, i: (bb, h, i)),
                  pl.BlockSpec((1, s, wq), lambda bb, h, i: (bb, 0, h)),
                  pl.BlockSpec((1, wv, s), lambda bb, h, i: (bb, h, 0)),
                  pl.BlockSpec(nw_col.shape, lambda bb, h, i: (0, 0))],
        out_specs=pl.BlockSpec((1, tq, wv), lambda bb, h, i: (bb, i, h)),
        out_shape=jax.ShapeDtypeStruct((b, s, nh * DIFF_DV), BF16),
        scratch_shapes=[pltpu.VMEM((hp, 1, 2 * tq), F32),
                        pltpu.VMEM((hp, DIFF_DV + BF16_SUBLANES, 2 * tq), F32),
                        pltpu.VMEM((hp, 2, tk, 2 * tq), F32), pltpu.VMEM((hp, 2, 1, 2 * tq), F32)],
        compiler_params=pltpu.CompilerParams(dimension_semantics=("parallel", "parallel", "arbitrary"),
                                             vmem_limit_bytes=VMEM_LIMIT),
        name="diff_attn",
    )(lam, qt, k3, vt, nw_col)


def _memkv_kernel(mem_ref, wk_ref, wv_ref, kt_ref, v_ref, *, scale):
    mb = mem_ref[0].astype(BF16)
    k = _dot(mb, wk_ref[...])
    kt_ref[0] = (k.T * scale).astype(BF16)
    v_ref[0] = _dot(mb, wv_ref[...]).astype(BF16)


def _memkv(mem, wk, wv, scale):
    b, m, d = mem.shape
    return pl.pallas_call(
        functools.partial(_memkv_kernel, scale=scale),
        grid=(b,),
        in_specs=[pl.BlockSpec((1, m, d), lambda bb: (bb, 0, 0)),
                  pl.BlockSpec(wk.shape, lambda bb: (0, 0)),
                  pl.BlockSpec(wv.shape, lambda bb: (0, 0))],
        out_specs=[pl.BlockSpec((1, d, m), lambda bb: (bb, 0, 0)),
                   pl.BlockSpec((1, m, d), lambda bb: (bb, 0, 0))],
        out_shape=[jax.ShapeDtypeStruct((b, d, m), BF16), jax.ShapeDtypeStruct((b, m, d), BF16)],
        compiler_params=pltpu.CompilerParams(dimension_semantics=("parallel",),
                                             vmem_limit_bytes=VMEM_LIMIT),
        name="memkv",
    )(mem, wk, wv)


def _mid_kernel(x_ref, og_ref, od_ref, wo1_ref, wo2_ref, g1_ref, b1_ref, wq_ref, kt_ref, v_ref, wo_ref,
                g2_ref, b2_ref, rw_ref, rw1_ref, rb_ref, x2_ref, x2p_ref, route_ref, gate_ref, cnt_ref, run_ref):
    i = pl.program_id(0)

    @pl.when(i == 0)
    def _():
        run_ref[...] = jnp.zeros_like(run_ref)

    tm, d = x_ref.shape
    sub = tm // MID_SUBBLOCKS
    spans = [(k * sub, (k + 1) * sub) for k in range(MID_SUBBLOCKS)]
    dh = d // XA_HEADS

    h = [_dot(og_ref[a:b, :], wo1_ref[...]) + _dot(od_ref[a:b, :], wo2_ref[...]) for a, b in spans]
    x1 = [_layer_norm(DEEPNORM_ALPHA * x_ref[a:b, :] + hk, g1_ref[...], b1_ref[...])
          for (a, b), hk in zip(spans, h)]
    q = [_dot(xk.astype(BF16), wq_ref[...]).astype(BF16) for xk in x1]
    heads = [[] for _ in spans]
    for hh in range(XA_HEADS):
        lo, hi = hh * dh, (hh + 1) * dh
        s = [_dot(qk[:, lo:hi], kt_ref[0, lo:hi, :]) for qk in q]
        p = [jnp.exp(sk - jnp.max(sk, axis=-1, keepdims=True)) for sk in s]
        p = [(pk / jnp.sum(pk, axis=-1, keepdims=True)).astype(BF16) for pk in p]
        for k, pk in enumerate(p):
            heads[k].append(_dot(pk, v_ref[0, :, lo:hi]))
    o = [jnp.concatenate(hk, axis=-1).astype(BF16) for hk in heads]
    h2 = [_dot(ok, wo_ref[...]) for ok in o]
    x2 = [_layer_norm(DEEPNORM_ALPHA * xk + hk, g2_ref[...], b2_ref[...]) for xk, hk in zip(x1, h2)]

    x_hi = [xk.astype(BF16) for xk in x2]
    x_lo = [(xk - hk.astype(F32)).astype(BF16) for xk, hk in zip(x2, x_hi)]
    r1 = [_dot(hk, rw_ref[...]) for hk in x_hi]
    r2 = [_dot(lk, rw1_ref[...]) for lk in x_lo]
    logits = [a1 + pltpu.roll(a1, LANES - N_EXPERTS, 1) + a2 + rb_ref[...] for a1, a2 in zip(r1, r2)]

    lane = lax.broadcasted_iota(I32, (sub, LANES), 1)
    rr = lax.broadcasted_iota(I32, (sub, sub), 0)
    cc = lax.broadcasted_iota(I32, (sub, sub), 1)
    tri = jnp.where(cc < rr, 1.0, 0.0).astype(BF16)
    run = run_ref[...]
    for (a, b), xk, lgk in zip(spans, x2, logits):
        x2_ref[a:b, :] = xk
        packed = _pack_bf16_pairs(xk)
        for piece in range(x2p_ref.shape[0]):
            x2p_ref[piece, a:b, :] = packed[:, piece * SC_ROW_WORDS:(piece + 1) * SC_ROW_WORDS]
        lg = jnp.where(lane < N_EXPERTS, lgk, NEG_BIG)
        sel = jnp.zeros((sub, LANES), F32)
        vals, hots, idxs = [], [], []
        for _ in range(TOP_K):
            mx = jnp.max(lg, axis=-1, keepdims=True)
            idx = jnp.min(jnp.where(lg == mx, lane, LANES), axis=-1, keepdims=True)
            hot = lane == idx
            vals.append(mx)
            idxs.append(idx)
            hots.append(hot)
            lg = jnp.where(hot, NEG_BIG, lg)
            sel = sel + jnp.where(hot, 1.0, 0.0)
        ex = [jnp.exp(v - vals[0]) for v in vals]
        den = ex[0] + ex[1] + ex[2] + ex[3]
        before = _dot(tri, sel.astype(BF16)) + run
        route = jnp.zeros((sub, LANES), F32)
        gates = jnp.zeros((sub, LANES), F32)
        for kk in range(TOP_K):
            rank = jnp.sum(jnp.where(hots[kk], before, 0.0), axis=-1, keepdims=True)
            route = jnp.where(lane == kk, idxs[kk].astype(F32), route)
            route = jnp.where(lane == TOP_K + kk, rank, route)
            gates = jnp.where(lane == kk, ex[kk] / den, gates)
        route_ref[:, a:b] = route.T[:2 * TOP_K].astype(I32)
        gate_ref[a:b, :] = gates
        run = run + jnp.sum(sel, axis=0, keepdims=True)
    run_ref[...] = run
    cnt_ref[...] = run.astype(I32)


def _mid(x2d, og, od, wo1, wo2, g1, b1, wq, kt, v, wo, g2, b2, rw, rw1, rb, tm, rows_per_batch):
    t, d = x2d.shape
    blocks_per_batch = rows_per_batch // tm
    pieces = (d // 2) // SC_ROW_WORDS
    row = lambda w: pl.BlockSpec((tm, w), lambda i: (i, 0))
    full = lambda a: pl.BlockSpec(a.shape, lambda i: (0, 0))
    per_batch = lambda a: pl.BlockSpec((1,) + a.shape[1:], lambda i: (i // blocks_per_batch, 0, 0))
    return pl.pallas_call(
        _mid_kernel,
        grid=(t // tm,),
        in_specs=[row(d), row(og.shape[1]), row(od.shape[1]), full(wo1), full(wo2), full(g1), full(b1),
                  full(wq), per_batch(kt), per_batch(v), full(wo), full(g2), full(b2), full(rw), full(rw1), full(rb)],
        out_specs=[row(d), pl.BlockSpec((pieces, tm, SC_ROW_WORDS), lambda i: (0, i, 0)),
                   pl.BlockSpec((2 * TOP_K, tm), lambda i: (0, i)), row(LANES),
                   pl.BlockSpec((1, LANES), lambda i: (0, 0))],
        out_shape=[jax.ShapeDtypeStruct((t, d), F32), jax.ShapeDtypeStruct((pieces, t, SC_ROW_WORDS), U32),
                   jax.ShapeDtypeStruct((2 * TOP_K, t), I32), jax.ShapeDtypeStruct((t, LANES), F32),
                   jax.ShapeDtypeStruct((1, LANES), I32)],
        scratch_shapes=[pltpu.VMEM((1, LANES), F32)],
        compiler_params=pltpu.CompilerParams(dimension_semantics=("arbitrary",),
                                             vmem_limit_bytes=VMEM_LIMIT),
        name="mid",
    )(x2d, og, od, wo1, wo2, g1, b1, wq, kt, v, wo, g2, b2, rw, rw1, rb)


def _sc_scatter_rows(rows, idx4, n_out):
    t, d = rows.shape
    mesh = plsc.VectorSubcoreMesh(core_axis_name="c", subcore_axis_name="s")

    @functools.partial(pl.kernel, out_type=jax.ShapeDtypeStruct((n_out, d), rows.dtype), mesh=mesh)
    def scatter_kernel(x_hbm, i0_hbm, i1_hbm, i2_hbm, i3_hbm, o_hbm):
        def body(x_vmem, i0, i1, i2, i3):
            for iv in (i0, i1, i2, i3):
                pltpu.sync_copy(x_vmem, o_hbm.at[iv.at[0]])

        ispec = pl.BlockSpec((1, SC_WINDOW), lambda i: (0, i))
        pltpu.emit_pipeline(
            body,
            grid=(t // SC_WINDOW,),
            in_specs=[pl.BlockSpec((SC_WINDOW, d), lambda i: (i, 0)), ispec, ispec, ispec, ispec],
            out_specs=[],
            core_axis_name=("c", "s"),
            dimension_semantics=(pltpu.PARALLEL,),
        )(x_hbm, i0_hbm, i1_hbm, i2_hbm, i3_hbm)

    return scatter_kernel(rows, *[idx4[k:k + 1] for k in range(TOP_K)])


def _sc_gather_rows(table, idx):
    n = idx.shape[0]
    d = table.shape[1]
    mesh = plsc.VectorSubcoreMesh(core_axis_name="c", subcore_axis_name="s")

    @functools.partial(pl.kernel, out_type=jax.ShapeDtypeStruct((n, d), table.dtype), mesh=mesh)
    def gather_kernel(x_hbm, i_hbm, o_hbm):
        def body(i_vmem, o_vmem):
            pltpu.sync_copy(x_hbm.at[i_vmem.at[0]], o_vmem)

        pltpu.emit_pipeline(
            body,
            grid=(n // SC_WINDOW,),
            in_specs=[pl.BlockSpec((1, SC_WINDOW), lambda i: (0, i))],
            out_specs=[pl.BlockSpec((SC_WINDOW, d), lambda i: (i, 0))],
            core_axis_name=("c", "s"),
            dimension_semantics=(pltpu.PARALLEL,),
        )(i_hbm, o_hbm)

    return gather_kernel(table, idx.reshape(1, n))


def _moe_kernel(be_ref, nu_ref, nx_ref, x_ref, wg_hbm, bg_ref, wu_hbm, bu_ref, wd_hbm, bd_ref, y_ref,
                stage_ref, wgb_ref, wub_ref, wdb_ref, sem):
    i = pl.program_id(0)
    used = i < nu_ref[0]
    expert = be_ref[i]
    prev = be_ref[jnp.maximum(i - 1, 0)]
    changed = jnp.logical_and(used, jnp.logical_or(i == 0, expert != prev))

    def weight_copies(e):
        return [pltpu.make_async_copy(w.at[e], stage_ref.at[k], sem.at[k])
                for k, w in enumerate((wg_hbm, wu_hbm, wd_hbm))]

    @pl.when(changed)
    def _():
        @pl.when(i == 0)
        def _():
            for cp in weight_copies(expert):
                cp.start()

        for cp in weight_copies(expert):
            cp.wait()
        wgb_ref[...] = stage_ref[0].astype(BF16)
        wub_ref[...] = stage_ref[1].astype(BF16)
        wdb_ref[...] = stage_ref[2].astype(BF16)
        nxt = nx_ref[i]

        @pl.when(nxt >= 0)
        def _():
            for cp in weight_copies(nxt):
                cp.start()

    @pl.when(used)
    def _():
        halves = [_unpack_bf16_pairs(x_ref[piece]) for piece in range(x_ref.shape[0])]
        x = jnp.concatenate([h[0] for h in halves] + [h[1] for h in halves], axis=-1).astype(BF16)
        hg = _dot(x, wgb_ref[...]) + bg_ref[0]
        hl = _dot(x, wub_ref[...]) + bu_ref[0]
        hg = jnp.minimum(hg, SWIGLU_LIMIT)
        hl = jnp.clip(hl, -SWIGLU_LIMIT, SWIGLU_LIMIT)
        act = hg * _sigmoid(SWIGLU_ALPHA * hg) * (hl + 1.0)
        y = _dot(act.astype(BF16), wdb_ref[...]) + bd_ref[0]
        packed = _pack_bf16_pairs(y)
        for piece in range(y_ref.shape[0]):
            y_ref[piece] = packed[:, piece * SC_ROW_WORDS:(piece + 1) * SC_ROW_WORDS]

    @pl.when(jnp.logical_not(used))
    def _():
        y_ref[...] = jnp.zeros_like(y_ref)


def _moe(block_e, n_used, next_e, xs, wg, bg, wu, bu, wd, bd, tm):
    pieces, p, dp = xs.shape
    d, dff = wg.shape[1], wg.shape[2]
    assert d == dff, "the shared f32 staging buffer assumes square expert matrices"
    n_blocks = p // tm
    bspec = lambda a: pl.BlockSpec((1,) + a.shape[1:], lambda i, be, nu, nx: (be[i], 0, 0))
    hbm = pl.BlockSpec(memory_space=pl.ANY)
    rows = pl.BlockSpec((pieces, tm, dp), lambda i, be, nu, nx: (0, i, 0))
    grid_spec = pltpu.PrefetchScalarGridSpec(
        num_scalar_prefetch=3,
        grid=(n_blocks,),
        in_specs=[rows, hbm, bspec(bg), hbm, bspec(bu), hbm, bspec(bd)],
        out_specs=rows,
        scratch_shapes=[pltpu.VMEM((3, d, dff), F32),
                        pltpu.VMEM((d, dff), BF16), pltpu.VMEM((d, dff), BF16), pltpu.VMEM((dff, d), BF16),
                        pltpu.SemaphoreType.DMA((3,))],
    )
    return pl.pallas_call(
        _moe_kernel,
        grid_spec=grid_spec,
        out_shape=jax.ShapeDtypeStruct((pieces, p, dp), U32),
        compiler_params=pltpu.CompilerParams(dimension_semantics=("arbitrary",),
                                             vmem_limit_bytes=VMEM_LIMIT),
        name="moe_ffn",
    )(block_e, n_used, next_e, xs, wg, bg, wu, bu, wd, bd)


def _final_kernel(x_ref, y4_ref, gate_ref, g_ref, b_ref, o_ref):
    pieces = y4_ref.shape[0]
    gates = gate_ref[...]
    lows = [0.0] * pieces
    highs = [0.0] * pieces
    for kk in range(TOP_K):
        gk = gates[:, kk:kk + 1]
        for piece in range(pieces):
            lo, hi = _unpack_bf16_pairs(y4_ref[piece, kk])
            lows[piece] = lows[piece] + gk * lo
            highs[piece] = highs[piece] + gk * hi
    y = jnp.concatenate(lows + highs, axis=-1)
    o_ref[...] = _layer_norm(DEEPNORM_ALPHA * x_ref[...] + y, g_ref[...], b_ref[...])


def _final(x2, y4, gates, g, b, tm):
    t, d = x2.shape
    row = lambda w: pl.BlockSpec((tm, w), lambda i: (i, 0))
    full = lambda a: pl.BlockSpec(a.shape, lambda i: (0, 0))
    return pl.pallas_call(
        _final_kernel,
        grid=(t // tm,),
        in_specs=[row(d), pl.BlockSpec(y4.shape[:2] + (tm, y4.shape[3]), lambda i: (0, 0, i, 0)),
                  row(gates.shape[1]), full(g), full(b)],
        out_specs=row(d),
        out_shape=jax.ShapeDtypeStruct((t, d), F32),
        compiler_params=pltpu.CompilerParams(dimension_semantics=("parallel",),
                                             vmem_limit_bytes=VMEM_LIMIT),
        name="final_ln",
    )(x2, y4, gates, g, b)


def _layout(route, counts, tm):
    t = route.shape[1]
    experts = route[:TOP_K]
    ranks = route[TOP_K:]
    counts = counts[0, :N_EXPERTS]
    padded = (counts + tm - 1) // tm * tm
    pad_end = jnp.cumsum(padded)
    pad_start = pad_end - padded
    eids = jnp.arange(N_EXPERTS, dtype=I32)[:, None, None]
    dest = ranks + jnp.sum(jnp.where(experts[None] == eids, pad_start[:, None, None], 0), axis=0)
    n_blocks = t * TOP_K // tm + N_EXPERTS
    starts = jnp.arange(n_blocks, dtype=I32) * tm
    block_e = jnp.minimum(jnp.sum((pad_end[None, :] <= starts[:, None]).astype(I32), axis=-1), N_EXPERTS - 1)
    n_used = (pad_end[-1] // tm).astype(I32).reshape(1)
    e_row = jnp.arange(N_EXPERTS, dtype=I32)
    later = (e_row[None, :] > e_row[:, None]) & (padded[None, :] > 0)
    next_of = jnp.min(jnp.where(later, e_row[None, :], N_EXPERTS), axis=1)
    next_of = jnp.where(next_of == N_EXPERTS, -1, next_of)
    next_e = jnp.sum(jnp.where(block_e[:, None] == e_row[None, :], next_of[None, :], 0), axis=1)
    return dest.astype(I32), block_e.astype(I32), n_used, next_e.astype(I32), n_blocks


def kernel(x, mem, w_in, gdn_conv_w, gdn_a_log, gdn_dt_bias, gdn_norm_w, diff_lq1, diff_lk1, diff_lq2, diff_lk2,
           diff_norm_w, w_out, ln1_g, ln1_b, xa_wq, xa_wk, xa_wv, xa_wo, ln2_g, ln2_b, router_w, router_b,
           exp_w_gate, exp_b_gate, exp_w_up, exp_b_up, exp_w_down, exp_b_down, ln3_g, ln3_b):
    b, s, d = x.shape
    t = b * s
    x2d = x.reshape(t, d)
    for l in range(DEPTH):
        lambda_init = 0.8 - 0.6 * math.exp(-0.3 * l)
        n_g = 4 * GDN_W
        w = w_in[l]
        wg = w[:, :n_g].astype(BF16)
        small = w[:, n_g:n_g + 2 * GDN_HEADS]
        ws = jnp.pad(small, ((0, 0), (0, SMALL_W - 2 * GDN_HEADS))).astype(BF16)
        d0 = n_g + 2 * GDN_HEADS
        wdq = w[:, d0:d0 + DIFF_W].astype(BF16)
        wdk = w[:, d0 + DIFF_W:d0 + 2 * DIFF_W].astype(BF16)
        wdv = w[:, d0 + 2 * DIFF_W:d0 + 3 * DIFF_W].astype(BF16)
        conv_w = gdn_conv_w[l].reshape(GDN_CONV, 3 * GDN_W).astype(F32)
        ab = jnp.zeros((2, SMALL_W), F32)
        ab = ab.at[0, GDN_HEADS:2 * GDN_HEADS].set(-jnp.exp(gdn_a_log[l].astype(F32)))
        ab = ab.at[1, GDN_HEADS:2 * GDN_HEADS].set(gdn_dt_bias[l].astype(F32))
        lam = (jnp.exp(jnp.sum(diff_lq1[l].astype(F32) * diff_lk1[l].astype(F32)))
               - jnp.exp(jnp.sum(diff_lq2[l].astype(F32) * diff_lk2[l].astype(F32))) + lambda_init).reshape(1)

        g2d, dqt, dk2d, dvt, s2d = _inproj(x2d, wg, wdq, wdk, wdv, ws, conv_w, tm=512, batch=b,
                                           q_scale=LOG2E * DIFF_DQK ** -0.5)
        o_gdn = _gdn(g2d.reshape(b, s, n_g), s2d.reshape(b, s, SMALL_W), ab,
                     gdn_norm_w[l].reshape(1, GDN_DV).astype(F32), ts=256)
        o_diff = _diff_attn(lam, dqt, dk2d.reshape(b, s, DIFF_W), dvt,
                            diff_norm_w[l].reshape(DIFF_DV, 1).astype(F32), tq=512, out_scale=1.0 - lambda_init)

        dh = d // XA_HEADS
        kt, v = _memkv(mem, xa_wk[l].astype(BF16), xa_wv[l].astype(BF16), scale=dh ** -0.5)
        wo_mix = w_out[l].astype(BF16)
        rw_f = router_w[l].astype(F32)
        rw_hi = rw_f.astype(BF16)
        rw_lo = (rw_f - rw_hi.astype(F32)).astype(BF16)
        rw = jnp.pad(jnp.concatenate([rw_hi, rw_lo], axis=1), ((0, 0), (0, LANES - 2 * N_EXPERTS)))
        rw1 = jnp.pad(rw_hi, ((0, 0), (0, LANES - N_EXPERTS)))
        rb = jnp.pad(router_b[l].astype(F32), (0, LANES - N_EXPERTS)).reshape(1, LANES)
        row = lambda a: a.reshape(1, -1).astype(F32)
        x2, x2p, route, gates, counts = _mid(
            x2d, o_gdn.reshape(t, GDN_W), o_diff.reshape(t, DIFF_HEADS * DIFF_DV),
            wo_mix[:GDN_W], wo_mix[GDN_W:], row(ln1_g[l]), row(ln1_b[l]),
            xa_wq[l].astype(BF16), kt, v, xa_wo[l].astype(BF16), row(ln2_g[l]), row(ln2_b[l]),
            rw, rw1, rb, tm=1024, rows_per_batch=s)

        tm_moe = 256
        dest, block_e, n_used, next_e, n_blocks = _layout(route, counts, tm_moe)
        pieces = x2p.shape[0]
        p_rows = n_blocks * tm_moe
        offs = (jnp.arange(pieces, dtype=I32) * p_rows)[:, None]
        idx_scatter = (dest[:, None, :] + offs[None]).reshape(TOP_K, pieces * t)
        xs = _sc_scatter_rows(x2p.reshape(pieces * t, SC_ROW_WORDS), idx_scatter,
                              pieces * p_rows).reshape(pieces, p_rows, SC_ROW_WORDS)
        ys = _moe(block_e, n_used, next_e, xs,
                  exp_w_gate[l], exp_b_gate[l].reshape(N_EXPERTS, 1, -1), exp_w_up[l],
                  exp_b_up[l].reshape(N_EXPERTS, 1, -1), exp_w_down[l], exp_b_down[l].reshape(N_EXPERTS, 1, -1),
                  tm=tm_moe)
        idx_gather = (dest.reshape(1, TOP_K * t) + offs).reshape(pieces * TOP_K * t)
        y4 = _sc_gather_rows(ys.reshape(pieces * p_rows, SC_ROW_WORDS),
                             idx_gather).reshape(pieces, TOP_K, t, SC_ROW_WORDS)
        x2d = _final(x2, y4, gates, row(ln3_g[l]), row(ln3_b[l]), tm=512)
    return x2d.reshape(b, s, d)
```

```python
import functools
import math

import jax
import jax.numpy as jnp
from jax import lax
from jax.experimental import pallas as pl
from jax.experimental.pallas import tpu as pltpu
from jax.experimental.pallas import tpu_sc as plsc

F32 = jnp.float32
BF16 = jnp.bfloat16
U32 = jnp.uint32
I32 = jnp.int32
HIGHEST = lax.Precision.HIGHEST

GDN_HEADS = 4
GDN_DK = 128
GDN_DV = 128
GDN_CONV = 4
GDN_CHUNK = 64
DIFF_HEADS = 4
DIFF_DQK = 64
DIFF_DV = 128
XA_HEADS = 4
N_EXPERTS = 32
TOP_K = 4
SWIGLU_ALPHA = 1.702
SWIGLU_LIMIT = 7.0
LN_EPS = 1e-5
RMS_EPS = 1e-6
DEPTH = 1
DEEPNORM_ALPHA = (2 * DEPTH) ** 0.25

GDN_W = GDN_HEADS * GDN_DK
DIFF_W = DIFF_HEADS * 2 * DIFF_DQK
SMALL_W = 128

LANES = 128
SUBLANES = 8
BF16_SUBLANES = 16
VMEM_LIMIT = 56 * 1024 * 1024
SC_WINDOW = 128
SC_ROW_WORDS = 256
DIFF_HEADS_PER_STEP = 2
DIFF_LANE_STRIP = 256
COMBINE_PARTS = 2
MID_SUBBLOCKS = 4

NEG_BIG = -1e30
LOG2E = 1.4426950408889634


def _dot(a, b):
    return jnp.dot(a, b, preferred_element_type=F32)


def _dot_nt(a, b):
    return lax.dot_general(a, b, (((1,), (1,)), ((), ())), preferred_element_type=F32)


def _dot_tn(a, b):
    return lax.dot_general(a, b, (((0,), (0,)), ((), ())), preferred_element_type=F32)


def _layer_norm(y, g, b):
    mu = jnp.mean(y, axis=-1, keepdims=True)
    d = y - mu
    var = jnp.mean(d * d, axis=-1, keepdims=True)
    return d * lax.rsqrt(var + LN_EPS) * g + b


def _sigmoid(x):
    return 1.0 / (1.0 + jnp.exp(-x))


def _pack_bf16_pairs(x):
    n = x.shape[1] // 2
    bits = pltpu.bitcast(x.astype(BF16).astype(F32), U32)
    return (bits[:, :n] >> 16) | bits[:, n:]


def _unpack_bf16_pairs(w):
    lo = pltpu.bitcast(w << 16, F32)
    hi = pltpu.bitcast(w & jnp.uint32(0xFFFF0000), F32)
    return lo, hi


def _inproj_kernel(x_ref, wg_ref, wq_ref, wk_ref, wv_ref, ws_ref, cw_ref, g_ref, qt_ref, k_ref, vt_ref, s_ref,
                   tail_ref, *, q_scale, per_batch):
    i = pl.program_id(0)
    tm = x_ref.shape[0]
    n_conv = 3 * GDN_W

    @pl.when(i == 0)
    def _():
        tail_ref[...] = jnp.zeros_like(tail_ref)

    xb = x_ref[...].astype(BF16)
    first = i % per_batch == 0
    slab = GDN_W
    row8 = lax.broadcasted_iota(I32, (SUBLANES, slab), 0)

    def proj_z_qt():
        g_ref[:, n_conv:] = _dot(xb, wg_ref[:, n_conv:]).astype(BF16)
        qt_ref[0] = (_dot(xb, wq_ref[...]) * q_scale).T.astype(BF16)

    def proj_k():
        k_ref[...] = _dot(xb, wk_ref[...]).astype(BF16)

    def proj_vt_s():
        vt_ref[0] = _dot(xb, wv_ref[...]).T.astype(BF16)
        s_ref[...] = _dot(xb, ws_ref[...])

    others = [proj_z_qt, proj_k, proj_vt_s]
    assert n_conv // slab == len(others)

    for c0 in range(0, n_conv, slab):
        pre = _dot(xb, wg_ref[:, c0:c0 + slab])
        others[c0 // slab]()
        halo = jnp.where(first, 0.0, tail_ref[:, c0:c0 + slab])
        tail_ref[:, c0:c0 + slab] = pre[tm - SUBLANES:, :]
        w = cw_ref[:, c0:c0 + slab]
        y = pre * w[GDN_CONV - 1:GDN_CONV, :]
        for j in range(1, GDN_CONV):
            xr = pltpu.roll(pre, j, 0)
            top = jnp.where(row8 < j, pltpu.roll(halo, j, 0), xr[0:SUBLANES])
            xr = jnp.concatenate([top, xr[SUBLANES:]], axis=0)
            y = y + xr * w[GDN_CONV - 1 - j:GDN_CONV - j, :]
        y = y * _sigmoid(y)
        for lo in range(0, slab, GDN_DK):
            yh = y[:, lo:lo + GDN_DK]
            if c0 < GDN_W:
                yh = yh * (lax.rsqrt(jnp.sum(yh * yh, axis=-1, keepdims=True) + RMS_EPS) * (GDN_DK ** -0.5))
            elif c0 < 2 * GDN_W:
                yh = yh * lax.rsqrt(jnp.sum(yh * yh, axis=-1, keepdims=True) + RMS_EPS)
            g_ref[:, c0 + lo:c0 + lo + GDN_DK] = yh.astype(BF16)


def _inproj(x2d, wg, wq, wk, wv, ws, conv_w, tm, batch, q_scale):
    t, d = x2d.shape
    s = t // batch
    per_batch = s // tm
    full = lambda a: pl.BlockSpec(a.shape, lambda i: (0, 0))
    row = lambda w: pl.BlockSpec((tm, w), lambda i: (i, 0))
    tr = lambda w: pl.BlockSpec((1, w, tm), lambda i: (i // per_batch, 0, i % per_batch))
    return pl.pallas_call(
        functools.partial(_inproj_kernel, q_scale=q_scale, per_batch=per_batch),
        grid=(t // tm,),
        in_specs=[row(d), full(wg), full(wq), full(wk), full(wv), full(ws), full(conv_w)],
        out_specs=[row(wg.shape[1]), tr(wq.shape[1]), row(wk.shape[1]), tr(wv.shape[1]), row(ws.shape[1])],
        out_shape=[jax.ShapeDtypeStruct((t, wg.shape[1]), BF16),
                   jax.ShapeDtypeStruct((batch, wq.shape[1], s), BF16),
                   jax.ShapeDtypeStruct((t, wk.shape[1]), BF16),
                   jax.ShapeDtypeStruct((batch, wv.shape[1], s), BF16),
                   jax.ShapeDtypeStruct((t, ws.shape[1]), F32)],
        scratch_shapes=[pltpu.VMEM((SUBLANES, conv_w.shape[1]), F32)],
        compiler_params=pltpu.CompilerParams(dimension_semantics=("arbitrary",),
                                             vmem_limit_bytes=VMEM_LIMIT),
        name="inproj",
    )(x2d, wg, wq, wk, wv, ws, conv_w)


def _gdn_kernel(q_ref, k_ref, v_ref, z_ref, s_ref, ab_ref, nw_ref, o_ref,
                state_ref, us_ref, wq_ref, ai_ref, kt_ref, gl_ref, *, ts):
    c_len = GDN_CHUNK
    n_chunks = ts // c_len
    nh = GDN_HEADS
    n = nh * n_chunks
    i = pl.program_id(1)
    w_slot = i % 2
    r_slot = 1 - w_slot

    @pl.when(i == 0)
    def _():
        state_ref[...] = jnp.zeros_like(state_ref)
        us_ref[...] = jnp.zeros_like(us_ref)
        wq_ref[...] = jnp.zeros_like(wq_ref)
        ai_ref[...] = jnp.zeros_like(ai_ref)
        kt_ref[...] = jnp.zeros_like(kt_ref)
        gl_ref[...] = jnp.zeros_like(gl_ref)

    nw = nw_ref[...]
    rec = {"states": [state_ref[h] for h in range(nh)], "outs": [[None] * n_chunks for _ in range(nh)]}
    r_wq = [wq_ref[r_slot, b] for b in range(n)]
    r_us = [us_ref[r_slot, b] for b in range(n)]
    r_ai = [ai_ref[r_slot, b] for b in range(n)]
    r_kt = [kt_ref[r_slot, b] for b in range(n)]
    r_gl = [gl_ref[r_slot, b] for b in range(n)]

    def rec_read_state(cc):
        idx = [h * n_chunks + cc for h in range(nh)]
        sb = [st.astype(BF16) for st in rec["states"]]
        ws_qs = [_dot(r_wq[idx[h]], sb[h]) for h in range(nh)]
        v_new = [r_us[idx[h]] - ws_qs[h][:c_len] for h in range(nh)]
        rec["ws_qs"] = ws_qs
        rec["vnb"] = [v.astype(BF16) for v in v_new]

    def rec_write_state(cc):
        idx = [h * n_chunks + cc for h in range(nh)]
        for h in range(nh):
            rec["outs"][h][cc] = rec["ws_qs"][h][c_len:] + _dot(r_ai[idx[h]], rec["vnb"][h])
        rec["states"] = [rec["states"][h] * r_gl[idx[h]] + _dot_tn(r_kt[idx[h]], rec["vnb"][h])
                         for h in range(nh)]

    def rec_finish():
        for h in range(nh):
            lo, hi = h * GDN_DK, (h + 1) * GDN_DK
            state_ref[h] = rec["states"][h]
            o = jnp.concatenate(rec["outs"][h], axis=0)
            o = o * lax.rsqrt(jnp.mean(o * o, axis=-1, keepdims=True) + RMS_EPS) * nw
            z = z_ref[0, :, lo:hi].astype(F32)
            o_ref[0, :, lo:hi] = (o * (z * _sigmoid(z))).astype(o_ref.dtype)

    rec_stages = []
    for cc in range(n_chunks):
        rec_stages += [functools.partial(rec_read_state, cc), functools.partial(rec_write_state, cc)]
    rec_stages.append(rec_finish)

    def weave():
        if rec_stages:
            rec_stages.pop(0)()

    qa = q_ref[0].astype(F32)
    ka = k_ref[0].astype(F32)
    va = v_ref[0].astype(F32)

    sg = s_ref[0]
    beta_all = _sigmoid(sg)
    sp_in = sg + ab_ref[1:2, :]
    softplus = jnp.maximum(sp_in, 0.0) + jnp.log(1.0 + jnp.exp(-jnp.abs(sp_in)))
    g_step = ab_ref[0:1, :] * softplus

    r = lax.broadcasted_iota(I32, (ts, ts), 0)
    c = lax.broadcasted_iota(I32, (ts, ts), 1)
    tri = jnp.where((r // c_len) == (c // c_len), jnp.where(c <= r, 1.0, 0.0), 0.0)
    gc = jnp.dot(tri, g_step, precision=HIGHEST, preferred_element_type=F32)
    gct = [gc[cc * c_len:(cc + 1) * c_len, :].T for cc in range(n_chunks)]

    ri = lax.broadcasted_iota(I32, (c_len, c_len), 0)
    ci = lax.broadcasted_iota(I32, (c_len, c_len), 1)
    causal = ci <= ri
    strict = ci < ri

    qs, ks, kbs, vbs, kbes, decays, qds = [], [], [], [], [], [], []
    for h in range(nh):
        lo, hi = h * GDN_DK, (h + 1) * GDN_DK
        qh = qa[:, lo:hi]
        kh = ka[:, lo:hi]
        vh = va[:, lo:hi]
        beta = beta_all[:, h:h + 1]
        gcol_all = gc[:, nh + h:nh + h + 1]
        for cc in range(n_chunks):
            b = h * n_chunks + cc
            r0, r1 = cc * c_len, (cc + 1) * c_len
            qc, kc, vc = qh[r0:r1], kh[r0:r1], vh[r0:r1]
            bcol = beta[r0:r1]
            gcol = gcol_all[r0:r1]
            grow = gct[cc][nh + h:nh + h + 1, :]
            decay = jnp.where(causal, jnp.exp(jnp.where(causal, gcol - grow, 0.0)), 0.0)
            eg = jnp.exp(gcol)
            g_last = gcol[c_len - 1:c_len, :]
            kb = kc * bcol
            qs.append(qc.astype(BF16))
            ks.append(kc.astype(BF16))
            kbs.append(kb)
            vbs.append(vc * bcol)
            kbes.append(kb * eg)
            decays.append(decay)
            qds.append((qc * eg).astype(BF16))
            kt_ref[w_slot, b] = (kc * jnp.exp(g_last - gcol)).astype(BF16)
            gl_ref[w_slot, b] = jnp.broadcast_to(jnp.exp(g_last), (1, GDN_DV))
        weave()

    kq = [_dot_nt(jnp.concatenate([kbs[b].astype(BF16), qs[b]], axis=0), ks[b]) for b in range(n)]
    ms = [-jnp.where(strict, kq[b][:c_len] * decays[b], 0.0) for b in range(n)]
    for b in range(n):
        ai_ref[w_slot, b] = jnp.where(causal, kq[b][c_len:] * decays[b], 0.0).astype(BF16)
    weave()
    ys = ms
    for _ in range(5):
        mb = [m.astype(BF16) for m in ms]
        ms = [_dot(mb[b], mb[b]) for b in range(n)]
        weave()
        mb = [m.astype(BF16) for m in ms]
        ys = [ys[b] + ms[b] + _dot(ys[b].astype(BF16), mb[b]) for b in range(n)]
        weave()
    rhs = [jnp.concatenate([vbs[b], kbes[b]], axis=1) for b in range(n)]
    for b in range(n):
        uw = rhs[b] + _dot(ys[b].astype(BF16), rhs[b].astype(BF16))
        us_ref[w_slot, b] = uw[:, :GDN_DV]
        wq_ref[w_slot, b] = jnp.concatenate([uw[:, GDN_DV:].astype(BF16), qds[b]], axis=0)
    while rec_stages:
        weave()


def _gdn(g3, s3, ab, nw, ts):
    b, s, _ = g3.shape
    n_t = s // ts
    n = GDN_HEADS * (ts // GDN_CHUNK)
    cur = lambda i: jnp.minimum(i, n_t - 1)
    prev = lambda i: jnp.maximum(i - 1, 0)

    return pl.pallas_call(
        functools.partial(_gdn_kernel, ts=ts),
        grid=(b, n_t + 1),
        in_specs=[pl.BlockSpec((1, ts, GDN_W), lambda bb, i: (bb, cur(i), 0)),
                  pl.BlockSpec((1, ts, GDN_W), lambda bb, i: (bb, cur(i), 1)),
                  pl.BlockSpec((1, ts, GDN_W), lambda bb, i: (bb, cur(i), 2)),
                  pl.BlockSpec((1, ts, GDN_W), lambda bb, i: (bb, prev(i), 3)),
                  pl.BlockSpec((1, ts, SMALL_W), lambda bb, i: (bb, cur(i), 0)),
                  pl.BlockSpec(ab.shape, lambda bb, i: (0, 0)),
                  pl.BlockSpec(nw.shape, lambda bb, i: (0, 0))],
        out_specs=pl.BlockSpec((1, ts, GDN_W), lambda bb, i: (bb, prev(i), 0)),
        out_shape=jax.ShapeDtypeStruct((b, s, GDN_W), BF16),
        scratch_shapes=[pltpu.VMEM((GDN_HEADS, GDN_DK, GDN_DV), F32),
                        pltpu.VMEM((2, n, GDN_CHUNK, GDN_DV), F32),
                        pltpu.VMEM((2, n, 2 * GDN_CHUNK, GDN_DV), BF16),
                        pltpu.VMEM((2, n, GDN_CHUNK, GDN_CHUNK), BF16),
                        pltpu.VMEM((2, n, GDN_CHUNK, GDN_DK), BF16),
                        pltpu.VMEM((2, n, 1, GDN_DV), F32)],
        compiler_params=pltpu.CompilerParams(dimension_semantics=("parallel", "arbitrary"),
                                             vmem_limit_bytes=VMEM_LIMIT),
        name="gdn",
    )(g3, g3, g3, g3, s3, ab, nw)


def _diff_kernel(lam_ref, qt_ref, k_ref, vt_ref, nw_ref, o_ref, m_ref, acc_ref, s_ref, cm_ref, *, tq, tk,
                 out_scale):
    i = pl.program_id(2)
    heads = range(m_ref.shape[0])
    dq2 = 2 * DIFF_DQK
    rowi = lax.broadcasted_iota(I32, (dq2, tq), 0)
    qq = []
    for hd in heads:
        qt = qt_ref[0, hd * dq2:(hd + 1) * dq2, :]
        zero = jnp.zeros_like(qt)
        qq.append(jnp.concatenate([jnp.where(rowi < DIFF_DQK, qt, zero),
                                   jnp.where(rowi >= DIFF_DQK, qt, zero)], axis=1))
    m_ref[...] = jnp.full(m_ref.shape, NEG_BIG, F32)
    acc_ref[...] = jnp.zeros(acc_ref.shape, F32)
    ones_rows = jnp.ones((BF16_SUBLANES, tk), BF16)
    strip = DIFF_LANE_STRIP

    def scores(key_block, slot, diag):
        start = pl.multiple_of(key_block * tk, tk)
        for hd in heads:
            kj = k_ref[0, pl.ds(start, tk), hd * dq2:(hd + 1) * dq2]
            for c0 in range(0, 2 * tq, strip):
                s = _dot(kj, qq[hd][:, c0:c0 + strip])
                if diag is not None:
                    kr = lax.broadcasted_iota(I32, (tk, strip), 0) + diag * tk
                    qc = lax.broadcasted_iota(I32, (tk, strip), 1) + c0 % tq
                    s = jnp.where(kr <= qc, s, NEG_BIG)
                s_ref[hd, slot, :, c0:c0 + strip] = s
                cm_ref[hd, slot, :, c0:c0 + strip] = jnp.max(s, axis=0, keepdims=True)

    def update(key_block, slot):
        start = pl.multiple_of(key_block * tk, tk)
        for hd in heads:
            vtj = vt_ref[0, hd * DIFF_DV:(hd + 1) * DIFF_DV, pl.ds(start, tk)]
            vt_ext = jnp.concatenate([vtj, ones_rows], axis=0)
            for c0 in range(0, 2 * tq, strip):
                m_old = m_ref[hd, :, c0:c0 + strip]
                m_new = jnp.maximum(m_old, cm_ref[hd, slot, :, c0:c0 + strip])
                alpha = jnp.exp2(m_old - m_new)
                p = jnp.exp2(s_ref[hd, slot, :, c0:c0 + strip] - m_new).astype(BF16)
                acc_ref[hd, :, c0:c0 + strip] = alpha * acc_ref[hd, :, c0:c0 + strip] + _dot(vt_ext, p)
                m_ref[hd, :, c0:c0 + strip] = m_new

    base = 2 * i
    scores(base, 0, 0)
    scores(base + 1, 1, 1)
    update(base, 0)

    def pair(jj, carry):
        prev = jnp.where(jj == 0, base + 1, 2 * jj - 1)
        scores(2 * jj, 0, None)
        update(prev, 1)
        scores(2 * jj + 1, 1, None)
        update(2 * jj, 0)
        return carry

    def two_pairs(jq, carry):
        pair(2 * jq, carry)
        pair(2 * jq + 1, carry)
        return carry

    lax.fori_loop(0, i // 2, two_pairs, 0)

    @pl.when(i % 2 == 1)
    def _():
        pair(i - 1, 0)

    update(jnp.where(i == 0, base + 1, base - 1), 1)

    for hd in heads:
        acc = acc_ref[hd]
        o = acc[:DIFF_DV] / acc[DIFF_DV:DIFF_DV + 1]
        od = o[:, :tq] - lam_ref[0] * o[:, tq:]
        od = od * lax.rsqrt(jnp.mean(od * od, axis=0, keepdims=True) + RMS_EPS) * nw_ref[...] * out_scale
        o_ref[0, :, hd * DIFF_DV:(hd + 1) * DIFF_DV] = od.T.astype(o_ref.dtype)


def _diff_attn(lam, qt, k3, vt, nw_col, tq, out_scale):
    b, s, _ = k3.shape
    nh = DIFF_HEADS
    hp = DIFF_HEADS_PER_STEP
    tk = tq // 2
    wq, wv = hp * 2 * DIFF_DQK, hp * DIFF_DV
    return pl.pallas_call(
        functools.partial(_diff_kernel, tq=tq, tk=tk, out_scale=out_scale),
        grid=(b, nh // hp, s // tq),
        in_specs=[pl.BlockSpec(memory_space=pltpu.SMEM),
                  pl.BlockSpec((1, wq, tq), lambda bb, h, i: (bb, h, i)),
                  pl.BlockSpec((1, s, wq), lambda bb, h, i: (bb, 0, h)),
                  pl.BlockSpec((1, wv, s), lambda bb, h, i: (bb, h, 0)),
                  pl.BlockSpec(nw_col.shape, lambda bb, h, i: (0, 0))],
        out_specs=pl.BlockSpec((1, tq, wv), lambda bb, h, i: (bb, i, h)),
        out_shape=jax.ShapeDtypeStruct((b, s, nh * DIFF_DV), BF16),
        scratch_shapes=[pltpu.VMEM((hp, 1, 2 * tq), F32),
                        pltpu.VMEM((hp, DIFF_DV + BF16_SUBLANES, 2 * tq), F32),
                        pltpu.VMEM((hp, 2, tk, 2 * tq), F32), pltpu.VMEM((hp, 2, 1, 2 * tq), F32)],
        compiler_params=pltpu.CompilerParams(dimension_semantics=("parallel", "parallel", "arbitrary"),
                                             vmem_limit_bytes=VMEM_LIMIT),
        name="diff_attn",
    )(lam, qt, k3, vt, nw_col)


def _memkv_kernel(mem_ref, wk_ref, wv_ref, kt_ref, v_ref, *, scale):
    mb = mem_ref[0].astype(BF16)
    k = _dot(mb, wk_ref[...])
    kt_ref[0] = (k.T * scale).astype(BF16)
    v_ref[0] = _dot(mb, wv_ref[...]).astype(BF16)


def _memkv(mem, wk, wv, scale):
    b, m, d = mem.shape
    return pl.pallas_call(
        functools.partial(_memkv_kernel, scale=scale),
        grid=(b,),
        in_specs=[pl.BlockSpec((1, m, d), lambda bb: (bb, 0, 0)),
                  pl.BlockSpec(wk.shape, lambda bb: (0, 0)),
                  pl.BlockSpec(wv.shape, lambda bb: (0, 0))],
        out_specs=[pl.BlockSpec((1, d, m), lambda bb: (bb, 0, 0)),
                   pl.BlockSpec((1, m, d), lambda bb: (bb, 0, 0))],
        out_shape=[jax.ShapeDtypeStruct((b, d, m), BF16), jax.ShapeDtypeStruct((b, m, d), BF16)],
        compiler_params=pltpu.CompilerParams(dimension_semantics=("parallel",),
                                             vmem_limit_bytes=VMEM_LIMIT),
        name="memkv",
    )(mem, wk, wv)


def _mid_kernel(x_ref, og_ref, od_ref, wo1_ref, wo2_ref, g1_ref, b1_ref, wq_ref, kt_ref, v_ref, wo_ref,
                g2_ref, b2_ref, rw_ref, rw1_ref, rb_ref, x2_ref, x2p_ref, route_ref, gate_ref, cnt_ref, run_ref):
    i = pl.program_id(0)

    @pl.when(i == 0)
    def _():
        run_ref[...] = jnp.zeros_like(run_ref)

    tm, d = x_ref.shape
    sub = tm // MID_SUBBLOCKS
    spans = [(k * sub, (k + 1) * sub) for k in range(MID_SUBBLOCKS)]
    dh = d // XA_HEADS

    h = [_dot(og_ref[a:b, :], wo1_ref[...]) + _dot(od_ref[a:b, :], wo2_ref[...]) for a, b in spans]
    x1 = [_layer_norm(DEEPNORM_ALPHA * x_ref[a:b, :] + hk, g1_ref[...], b1_ref[...])
          for (a, b), hk in zip(spans, h)]
    q = [_dot(xk.astype(BF16), wq_ref[...]).astype(BF16) for xk in x1]
    heads = [[] for _ in spans]
    for hh in range(XA_HEADS):
        lo, hi = hh * dh, (hh + 1) * dh
        s = [_dot(qk[:, lo:hi], kt_ref[0, lo:hi, :]) for qk in q]
        p = [jnp.exp(sk - jnp.max(sk, axis=-1, keepdims=True)) for sk in s]
        p = [(pk / jnp.sum(pk, axis=-1, keepdims=True)).astype(BF16) for pk in p]
        for k, pk in enumerate(p):
            heads[k].append(_dot(pk, v_ref[0, :, lo:hi]))
    o = [jnp.concatenate(hk, axis=-1).astype(BF16) for hk in heads]
    h2 = [_dot(ok, wo_ref[...]) for ok in o]
    x2 = [_layer_norm(DEEPNORM_ALPHA * xk + hk, g2_ref[...], b2_ref[...]) for xk, hk in zip(x1, h2)]

    x_hi = [xk.astype(BF16) for xk in x2]
    x_lo = [(xk - hk.astype(F32)).astype(BF16) for xk, hk in zip(x2, x_hi)]
    r1 = [_dot(hk, rw_ref[...]) for hk in x_hi]
    r2 = [_dot(lk, rw1_ref[...]) for lk in x_lo]
    logits = [a1 + pltpu.roll(a1, LANES - N_EXPERTS, 1) + a2 + rb_ref[...] for a1, a2 in zip(r1, r2)]

    lane = lax.broadcasted_iota(I32, (sub, LANES), 1)
    rr = lax.broadcasted_iota(I32, (sub, sub), 0)
    cc = lax.broadcasted_iota(I32, (sub, sub), 1)
    tri = jnp.where(cc < rr, 1.0, 0.0).astype(BF16)
    run = run_ref[...]
    for (a, b), xk, lgk in zip(spans, x2, logits):
        x2_ref[a:b, :] = xk
        packed = _pack_bf16_pairs(xk)
        for piece in range(x2p_ref.shape[0]):
            x2p_ref[piece, a:b, :] = packed[:, piece * SC_ROW_WORDS:(piece + 1) * SC_ROW_WORDS]
        lg = jnp.where(lane < N_EXPERTS, lgk, NEG_BIG)
        sel = jnp.zeros((sub, LANES), F32)
        vals, hots, idxs = [], [], []
        for _ in range(TOP_K):
            mx = jnp.max(lg, axis=-1, keepdims=True)
            idx = jnp.min(jnp.where(lg == mx, lane, LANES), axis=-1, keepdims=True)
            hot = lane == idx
            vals.append(mx)
            idxs.append(idx)
            hots.append(hot)
            lg = jnp.where(hot, NEG_BIG, lg)
            sel = sel + jnp.where(hot, 1.0, 0.0)
        ex = [jnp.exp(v - vals[0]) for v in vals]
        den = ex[0] + ex[1] + ex[2] + ex[3]
        before = _dot(tri, sel.astype(BF16)) + run
        route = jnp.zeros((sub, LANES), F32)
        gates = jnp.zeros((sub, LANES), F32)
        for kk in range(TOP_K):
            rank = jnp.sum(jnp.where(hots[kk], before, 0.0), axis=-1, keepdims=True)
            route = jnp.where(lane == kk, idxs[kk].astype(F32), route)
            route = jnp.where(lane == TOP_K + kk, rank, route)
            gates = jnp.where(lane == kk, ex[kk] / den, gates)
        route_ref[:, a:b] = route.T[:2 * TOP_K].astype(I32)
        gate_ref[a:b, :] = gates
        run = run + jnp.sum(sel, axis=0, keepdims=True)
    run_ref[...] = run
    cnt_ref[...] = run.astype(I32)


def _mid(x2d, og, od, wo1, wo2, g1, b1, wq, kt, v, wo, g2, b2, rw, rw1, rb, tm, rows_per_batch):
    t, d = x2d.shape
    blocks_per_batch = rows_per_batch // tm
    pieces = (d // 2) // SC_ROW_WORDS
    row = lambda w: pl.BlockSpec((tm, w), lambda i: (i, 0))
    full = lambda a: pl.BlockSpec(a.shape, lambda i: (0, 0))
    per_batch = lambda a: pl.BlockSpec((1,) + a.shape[1:], lambda i: (i // blocks_per_batch, 0, 0))
    return pl.pallas_call(
        _mid_kernel,
        grid=(t // tm,),
        in_specs=[row(d), row(og.shape[1]), row(od.shape[1]), full(wo1), full(wo2), full(g1), full(b1),
                  full(wq), per_batch(kt), per_batch(v), full(wo), full(g2), full(b2), full(rw), full(rw1), full(rb)],
        out_specs=[row(d), pl.BlockSpec((pieces, tm, SC_ROW_WORDS), lambda i: (0, i, 0)),
                   pl.BlockSpec((2 * TOP_K, tm), lambda i: (0, i)), row(LANES),
                   pl.BlockSpec((1, LANES), lambda i: (0, 0))],
        out_shape=[jax.ShapeDtypeStruct((t, d), F32), jax.ShapeDtypeStruct((pieces, t, SC_ROW_WORDS), U32),
                   jax.ShapeDtypeStruct((2 * TOP_K, t), I32), jax.ShapeDtypeStruct((t, LANES), F32),
                   jax.ShapeDtypeStruct((1, LANES), I32)],
        scratch_shapes=[pltpu.VMEM((1, LANES), F32)],
        compiler_params=pltpu.CompilerParams(dimension_semantics=("arbitrary",),
                                             vmem_limit_bytes=VMEM_LIMIT),
        name="mid",
    )(x2d, og, od, wo1, wo2, g1, b1, wq, kt, v, wo, g2, b2, rw, rw1, rb)


def _sc_scatter_rows(rows, idx4, n_out):
    t, d = rows.shape
    mesh = plsc.VectorSubcoreMesh(core_axis_name="c", subcore_axis_name="s")

    @functools.partial(pl.kernel, out_type=jax.ShapeDtypeStruct((n_out, d), rows.dtype), mesh=mesh)
    def scatter_kernel(x_hbm, i0_hbm, i1_hbm, i2_hbm, i3_hbm, o_hbm):
        def body(x_vmem, i0, i1, i2, i3):
            for iv in (i0, i1, i2, i3):
                pltpu.sync_copy(x_vmem, o_hbm.at[iv.at[0]])

        ispec = pl.BlockSpec((1, SC_WINDOW), lambda i: (0, i))
        pltpu.emit_pipeline(
            body,
            grid=(t // SC_WINDOW,),
            in_specs=[pl.BlockSpec((SC_WINDOW, d), lambda i: (i, 0)), ispec, ispec, ispec, ispec],
            out_specs=[],
            core_axis_name=("c", "s"),
            dimension_semantics=(pltpu.PARALLEL,),
        )(x_hbm, i0_hbm, i1_hbm, i2_hbm, i3_hbm)

    return scatter_kernel(rows, *[idx4[k:k + 1] for k in range(TOP_K)])


def _sc_gather_rows(table, idx):
    n = idx.shape[0]
    d = table.shape[1]
    mesh = plsc.VectorSubcoreMesh(core_axis_name="c", subcore_axis_name="s")

    @functools.partial(pl.kernel, out_type=jax.ShapeDtypeStruct((n, d), table.dtype), mesh=mesh)
    def gather_kernel(x_hbm, i_hbm, o_hbm):
        def body(i_vmem, o_vmem):
            pltpu.sync_copy(x_hbm.at[i_vmem.at[0]], o_vmem)

        pltpu.emit_pipeline(
            body,
            grid=(n // SC_WINDOW,),
            in_specs=[pl.BlockSpec((1, SC_WINDOW), lambda i: (0, i))],
            out_specs=[pl.BlockSpec((SC_WINDOW, d), lambda i: (i, 0))],
            core_axis_name=("c", "s"),
            dimension_semantics=(pltpu.PARALLEL,),
        )(i_hbm, o_hbm)

    return gather_kernel(table, idx.reshape(1, n))


def _moe_kernel(be_ref, nu_ref, nx_ref, x_ref, wg_hbm, bg_ref, wu_hbm, bu_ref, wd_hbm, bd_ref, y_ref,
                stage_ref, wgb_ref, wub_ref, wdb_ref, sem):
    i = pl.program_id(0)
    used = i < nu_ref[0]
    expert = be_ref[i]
    prev = be_ref[jnp.maximum(i - 1, 0)]
    changed = jnp.logical_and(used, jnp.logical_or(i == 0, expert != prev))

    def weight_copies(e):
        return [pltpu.make_async_copy(w.at[e], stage_ref.at[k], sem.at[k])
                for k, w in enumerate((wg_hbm, wu_hbm, wd_hbm))]

    @pl.when(changed)
    def _():
        @pl.when(i == 0)
        def _():
            for cp in weight_copies(expert):
                cp.start()

        for cp in weight_copies(expert):
            cp.wait()
        wgb_ref[...] = stage_ref[0].astype(BF16)
        wub_ref[...] = stage_ref[1].astype(BF16)
        wdb_ref[...] = stage_ref[2].astype(BF16)
        nxt = nx_ref[i]

        @pl.when(nxt >= 0)
        def _():
            for cp in weight_copies(nxt):
                cp.start()

    @pl.when(used)
    def _():
        halves = [_unpack_bf16_pairs(x_ref[piece]) for piece in range(x_ref.shape[0])]
        x = jnp.concatenate([h[0] for h in halves] + [h[1] for h in halves], axis=-1).astype(BF16)
        hg = _dot(x, wgb_ref[...]) + bg_ref[0]
        hl = _dot(x, wub_ref[...]) + bu_ref[0]
        hg = jnp.minimum(hg, SWIGLU_LIMIT)
        hl = jnp.clip(hl, -SWIGLU_LIMIT, SWIGLU_LIMIT)
        act = hg * _sigmoid(SWIGLU_ALPHA * hg) * (hl + 1.0)
        y = _dot(act.astype(BF16), wdb_ref[...]) + bd_ref[0]
        packed = _pack_bf16_pairs(y)
        for piece in range(y_ref.shape[0]):
            y_ref[piece] = packed[:, piece * SC_ROW_WORDS:(piece + 1) * SC_ROW_WORDS]

    @pl.when(jnp.logical_not(used))
    def _():
        y_ref[...] = jnp.zeros_like(y_ref)


def _moe(block_e, n_used, next_e, xs, wg, bg, wu, bu, wd, bd, tm):
    pieces, p, dp = xs.shape
    d, dff = wg.shape[1], wg.shape[2]
    assert d == dff, "the shared f32 staging buffer assumes square expert matrices"
    n_blocks = p // tm
    bspec = lambda a: pl.BlockSpec((1,) + a.shape[1:], lambda i, be, nu, nx: (be[i], 0, 0))
    hbm = pl.BlockSpec(memory_space=pl.ANY)
    rows = pl.BlockSpec((pieces, tm, dp), lambda i, be, nu, nx: (0, i, 0))
    grid_spec = pltpu.PrefetchScalarGridSpec(
        num_scalar_prefetch=3,
        grid=(n_blocks,),
        in_specs=[rows, hbm, bspec(bg), hbm, bspec(bu), hbm, bspec(bd)],
        out_specs=rows,
        scratch_shapes=[pltpu.VMEM((3, d, dff), F32),
                        pltpu.VMEM((d, dff), BF16), pltpu.VMEM((d, dff), BF16), pltpu.VMEM((dff, d), BF16),
                        pltpu.SemaphoreType.DMA((3,))],
    )
    return pl.pallas_call(
        _moe_kernel,
        grid_spec=grid_spec,
        out_shape=jax.ShapeDtypeStruct((pieces, p, dp), U32),
        compiler_params=pltpu.CompilerParams(dimension_semantics=("arbitrary",),
                                             vmem_limit_bytes=VMEM_LIMIT),
        name="moe_ffn",
    )(block_e, n_used, next_e, xs, wg, bg, wu, bu, wd, bd)


def _final_kernel(x_ref, y4_ref, gate_ref, g_ref, b_ref, *prev_and_out):
    o_ref = prev_and_out[-1]
    pieces = y4_ref.shape[0]
    gates = gate_ref[...]
    lows = [0.0] * pieces
    highs = [0.0] * pieces
    for kk in range(TOP_K):
        gk = gates[:, kk:kk + 1]
        for piece in range(pieces):
            lo, hi = _unpack_bf16_pairs(y4_ref[piece, kk])
            lows[piece] = lows[piece] + gk * lo
            highs[piece] = highs[piece] + gk * hi
    y = jnp.concatenate(lows + highs, axis=-1)
    o_ref[...] = _layer_norm(DEEPNORM_ALPHA * x_ref[...] + y, g_ref[...], b_ref[...])


def _final(x2, y4, gates, g, b, tm, part, n_parts, prev):
    t, d = x2.shape
    blocks = t // n_parts // tm
    off = part * blocks
    row = lambda w: pl.BlockSpec((tm, w), lambda i: (i + off, 0))
    full = lambda a: pl.BlockSpec(a.shape, lambda i: (0, 0))
    in_specs = [row(d), pl.BlockSpec(y4.shape[:2] + (tm, y4.shape[3]), lambda i: (0, 0, i, 0)),
                row(gates.shape[1]), full(g), full(b)]
    args = [x2, y4, gates, g, b]
    aliases = {}
    if prev is not None:
        in_specs.append(pl.BlockSpec(memory_space=pl.ANY))
        args.append(prev)
        aliases = {len(args) - 1: 0}
    return pl.pallas_call(
        _final_kernel,
        grid=(blocks,),
        in_specs=in_specs,
        out_specs=row(d),
        out_shape=jax.ShapeDtypeStruct((t, d), F32),
        input_output_aliases=aliases,
        compiler_params=pltpu.CompilerParams(dimension_semantics=("parallel",),
                                             vmem_limit_bytes=VMEM_LIMIT),
        name="final_ln",
    )(*args)


def _layout(route, counts, tm):
    t = route.shape[1]
    experts = route[:TOP_K]
    ranks = route[TOP_K:]
    counts = counts[0, :N_EXPERTS]
    padded = (counts + tm - 1) // tm * tm
    pad_end = jnp.cumsum(padded)
    pad_start = pad_end - padded
    eids = jnp.arange(N_EXPERTS, dtype=I32)[:, None, None]
    dest = ranks + jnp.sum(jnp.where(experts[None] == eids, pad_start[:, None, None], 0), axis=0)
    n_blocks = t * TOP_K // tm + N_EXPERTS
    starts = jnp.arange(n_blocks, dtype=I32) * tm
    block_e = jnp.minimum(jnp.sum((pad_end[None, :] <= starts[:, None]).astype(I32), axis=-1), N_EXPERTS - 1)
    n_used = (pad_end[-1] // tm).astype(I32).reshape(1)
    e_row = jnp.arange(N_EXPERTS, dtype=I32)
    later = (e_row[None, :] > e_row[:, None]) & (padded[None, :] > 0)
    next_of = jnp.min(jnp.where(later, e_row[None, :], N_EXPERTS), axis=1)
    next_of = jnp.where(next_of == N_EXPERTS, -1, next_of)
    next_e = jnp.sum(jnp.where(block_e[:, None] == e_row[None, :], next_of[None, :], 0), axis=1)
    return dest.astype(I32), block_e.astype(I32), n_used, next_e.astype(I32), n_blocks


def kernel(x, mem, w_in, gdn_conv_w, gdn_a_log, gdn_dt_bias, gdn_norm_w, diff_lq1, diff_lk1, diff_lq2, diff_lk2,
           diff_norm_w, w_out, ln1_g, ln1_b, xa_wq, xa_wk, xa_wv, xa_wo, ln2_g, ln2_b, router_w, router_b,
           exp_w_gate, exp_b_gate, exp_w_up, exp_b_up, exp_w_down, exp_b_down, ln3_g, ln3_b):
    b, s, d = x.shape
    t = b * s
    x2d = x.reshape(t, d)
    for l in range(DEPTH):
        lambda_init = 0.8 - 0.6 * math.exp(-0.3 * l)
        n_g = 4 * GDN_W
        w = w_in[l]
        wg = w[:, :n_g].astype(BF16)
        small = w[:, n_g:n_g + 2 * GDN_HEADS]
        ws = jnp.pad(small, ((0, 0), (0, SMALL_W - 2 * GDN_HEADS))).astype(BF16)
        d0 = n_g + 2 * GDN_HEADS
        wdq = w[:, d0:d0 + DIFF_W].astype(BF16)
        wdk = w[:, d0 + DIFF_W:d0 + 2 * DIFF_W].astype(BF16)
        wdv = w[:, d0 + 2 * DIFF_W:d0 + 3 * DIFF_W].astype(BF16)
        conv_w = gdn_conv_w[l].reshape(GDN_CONV, 3 * GDN_W).astype(F32)
        ab = jnp.zeros((2, SMALL_W), F32)
        ab = ab.at[0, GDN_HEADS:2 * GDN_HEADS].set(-jnp.exp(gdn_a_log[l].astype(F32)))
        ab = ab.at[1, GDN_HEADS:2 * GDN_HEADS].set(gdn_dt_bias[l].astype(F32))
        lam = (jnp.exp(jnp.sum(diff_lq1[l].astype(F32) * diff_lk1[l].astype(F32)))
               - jnp.exp(jnp.sum(diff_lq2[l].astype(F32) * diff_lk2[l].astype(F32))) + lambda_init).reshape(1)

        g2d, dqt, dk2d, dvt, s2d = _inproj(x2d, wg, wdq, wdk, wdv, ws, conv_w, tm=512, batch=b,
                                           q_scale=LOG2E * DIFF_DQK ** -0.5)
        o_gdn = _gdn(g2d.reshape(b, s, n_g), s2d.reshape(b, s, SMALL_W), ab,
                     gdn_norm_w[l].reshape(1, GDN_DV).astype(F32), ts=256)
        o_diff = _diff_attn(lam, dqt, dk2d.reshape(b, s, DIFF_W), dvt,
                            diff_norm_w[l].reshape(DIFF_DV, 1).astype(F32), tq=512, out_scale=1.0 - lambda_init)

        dh = d // XA_HEADS
        kt, v = _memkv(mem, xa_wk[l].astype(BF16), xa_wv[l].astype(BF16), scale=dh ** -0.5)
        wo_mix = w_out[l].astype(BF16)
        rw_f = router_w[l].astype(F32)
        rw_hi = rw_f.astype(BF16)
        rw_lo = (rw_f - rw_hi.astype(F32)).astype(BF16)
        rw = jnp.pad(jnp.concatenate([rw_hi, rw_lo], axis=1), ((0, 0), (0, LANES - 2 * N_EXPERTS)))
        rw1 = jnp.pad(rw_hi, ((0, 0), (0, LANES - N_EXPERTS)))
        rb = jnp.pad(router_b[l].astype(F32), (0, LANES - N_EXPERTS)).reshape(1, LANES)
        row = lambda a: a.reshape(1, -1).astype(F32)
        x2, x2p, route, gates, counts = _mid(
            x2d, o_gdn.reshape(t, GDN_W), o_diff.reshape(t, DIFF_HEADS * DIFF_DV),
            wo_mix[:GDN_W], wo_mix[GDN_W:], row(ln1_g[l]), row(ln1_b[l]),
            xa_wq[l].astype(BF16), kt, v, xa_wo[l].astype(BF16), row(ln2_g[l]), row(ln2_b[l]),
            rw, rw1, rb, tm=1024, rows_per_batch=s)

        tm_moe = 256
        dest, block_e, n_used, next_e, n_blocks = _layout(route, counts, tm_moe)
        pieces = x2p.shape[0]
        p_rows = n_blocks * tm_moe
        offs = (jnp.arange(pieces, dtype=I32) * p_rows)[:, None]
        idx_scatter = (dest[:, None, :] + offs[None]).reshape(TOP_K, pieces * t)
        xs = _sc_scatter_rows(x2p.reshape(pieces * t, SC_ROW_WORDS), idx_scatter,
                              pieces * p_rows).reshape(pieces, p_rows, SC_ROW_WORDS)
        ys = _moe(block_e, n_used, next_e, xs,
                  exp_w_gate[l], exp_b_gate[l].reshape(N_EXPERTS, 1, -1), exp_w_up[l],
                  exp_b_up[l].reshape(N_EXPERTS, 1, -1), exp_w_down[l], exp_b_down[l].reshape(N_EXPERTS, 1, -1),
                  tm=tm_moe)
        tp = t // COMBINE_PARTS
        y4 = []
        for part in range(COMBINE_PARTS):
            idx_gather = (dest[:, part * tp:(part + 1) * tp].reshape(1, TOP_K * tp) + offs).reshape(-1)
            y4.append(_sc_gather_rows(ys.reshape(pieces * p_rows, SC_ROW_WORDS),
                                      idx_gather).reshape(pieces, TOP_K, tp, SC_ROW_WORDS))
        out = None
        for part in range(COMBINE_PARTS):
            out = _final(x2, y4[part], gates, row(ln3_g[l]), row(ln3_b[l]), tm=512, part=part,
                         n_parts=COMBINE_PARTS, prev=out)
        x2d = out
    return x2d.reshape(b, s, d)
```

```python
import functools
import math

import jax
import jax.numpy as jnp
from jax import lax
from jax.experimental import pallas as pl
from jax.experimental.pallas import tpu as pltpu
from jax.experimental.pallas import tpu_sc as plsc

F32 = jnp.float32
BF16 = jnp.bfloat16
U32 = jnp.uint32
I32 = jnp.int32
HIGHEST = lax.Precision.HIGHEST

GDN_HEADS = 4
GDN_DK = 128
GDN_DV = 128
GDN_CONV = 4
GDN_CHUNK = 64
DIFF_HEADS = 4
DIFF_DQK = 64
DIFF_DV = 128
XA_HEADS = 4
N_EXPERTS = 32
TOP_K = 4
SWIGLU_ALPHA = 1.702
SWIGLU_LIMIT = 7.0
LN_EPS = 1e-5
RMS_EPS = 1e-6
DEPTH = 1
DEEPNORM_ALPHA = (2 * DEPTH) ** 0.25

GDN_W = GDN_HEADS * GDN_DK
DIFF_W = DIFF_HEADS * 2 * DIFF_DQK
SMALL_W = 128

LANES = 128
SUBLANES = 8
BF16_SUBLANES = 16
VMEM_LIMIT = 56 * 1024 * 1024
SC_WINDOW = 128
SC_ROW_WORDS = 256
DIFF_HEADS_PER_STEP = 2
DIFF_LANE_STRIP = 256
WEIGHT_DMA_PRIORITY = 1
MID_SUBBLOCKS = 4

NEG_BIG = -1e30
LOG2E = 1.4426950408889634


def _dot(a, b):
    return jnp.dot(a, b, preferred_element_type=F32)


def _dot_nt(a, b):
    return lax.dot_general(a, b, (((1,), (1,)), ((), ())), preferred_element_type=F32)


def _dot_tn(a, b):
    return lax.dot_general(a, b, (((0,), (0,)), ((), ())), preferred_element_type=F32)


def _layer_norm(y, g, b):
    mu = jnp.mean(y, axis=-1, keepdims=True)
    d = y - mu
    var = jnp.mean(d * d, axis=-1, keepdims=True)
    return d * lax.rsqrt(var + LN_EPS) * g + b


def _sigmoid(x):
    return 1.0 / (1.0 + jnp.exp(-x))


def _pack_bf16_pairs(x):
    n = x.shape[1] // 2
    bits = pltpu.bitcast(x.astype(BF16).astype(F32), U32)
    return (bits[:, :n] >> 16) | bits[:, n:]


def _unpack_bf16_pairs(w):
    lo = pltpu.bitcast(w << 16, F32)
    hi = pltpu.bitcast(w & jnp.uint32(0xFFFF0000), F32)
    return lo, hi


def _inproj_kernel(x_ref, wg_ref, wq_ref, wk_ref, wv_ref, ws_ref, cw_ref, g_ref, qt_ref, k_ref, vt_ref, s_ref,
                   tail_ref, *, q_scale, per_batch):
    i = pl.program_id(0)
    tm = x_ref.shape[0]
    n_conv = 3 * GDN_W

    @pl.when(i == 0)
    def _():
        tail_ref[...] = jnp.zeros_like(tail_ref)

    xb = x_ref[...].astype(BF16)
    first = i % per_batch == 0
    slab = GDN_W
    row8 = lax.broadcasted_iota(I32, (SUBLANES, slab), 0)

    def proj_z_qt():
        g_ref[:, n_conv:] = _dot(xb, wg_ref[:, n_conv:]).astype(BF16)
        qt_ref[0] = (_dot(xb, wq_ref[...]) * q_scale).T.astype(BF16)

    def proj_k():
        k_ref[...] = _dot(xb, wk_ref[...]).astype(BF16)

    def proj_vt_s():
        vt_ref[0] = _dot(xb, wv_ref[...]).T.astype(BF16)
        s_ref[...] = _dot(xb, ws_ref[...])

    others = [proj_z_qt, proj_k, proj_vt_s]
    assert n_conv // slab == len(others)

    for c0 in range(0, n_conv, slab):
        pre = _dot(xb, wg_ref[:, c0:c0 + slab])
        others[c0 // slab]()
        halo = jnp.where(first, 0.0, tail_ref[:, c0:c0 + slab])
        tail_ref[:, c0:c0 + slab] = pre[tm - SUBLANES:, :]
        w = cw_ref[:, c0:c0 + slab]
        y = pre * w[GDN_CONV - 1:GDN_CONV, :]
        for j in range(1, GDN_CONV):
            xr = pltpu.roll(pre, j, 0)
            top = jnp.where(row8 < j, pltpu.roll(halo, j, 0), xr[0:SUBLANES])
            xr = jnp.concatenate([top, xr[SUBLANES:]], axis=0)
            y = y + xr * w[GDN_CONV - 1 - j:GDN_CONV - j, :]
        y = y * _sigmoid(y)
        for lo in range(0, slab, GDN_DK):
            yh = y[:, lo:lo + GDN_DK]
            if c0 < GDN_W:
                yh = yh * (lax.rsqrt(jnp.sum(yh * yh, axis=-1, keepdims=True) + RMS_EPS) * (GDN_DK ** -0.5))
            elif c0 < 2 * GDN_W:
                yh = yh * lax.rsqrt(jnp.sum(yh * yh, axis=-1, keepdims=True) + RMS_EPS)
            g_ref[:, c0 + lo:c0 + lo + GDN_DK] = yh.astype(BF16)


def _inproj(x2d, wg, wq, wk, wv, ws, conv_w, tm, batch, q_scale):
    t, d = x2d.shape
    s = t // batch
    per_batch = s // tm
    full = lambda a: pl.BlockSpec(a.shape, lambda i: (0, 0))
    row = lambda w: pl.BlockSpec((tm, w), lambda i: (i, 0))
    tr = lambda w: pl.BlockSpec((1, w, tm), lambda i: (i // per_batch, 0, i % per_batch))
    return pl.pallas_call(
        functools.partial(_inproj_kernel, q_scale=q_scale, per_batch=per_batch),
        grid=(t // tm,),
        in_specs=[row(d), full(wg), full(wq), full(wk), full(wv), full(ws), full(conv_w)],
        out_specs=[row(wg.shape[1]), tr(wq.shape[1]), row(wk.shape[1]), tr(wv.shape[1]), row(ws.shape[1])],
        out_shape=[jax.ShapeDtypeStruct((t, wg.shape[1]), BF16),
                   jax.ShapeDtypeStruct((batch, wq.shape[1], s), BF16),
                   jax.ShapeDtypeStruct((t, wk.shape[1]), BF16),
                   jax.ShapeDtypeStruct((batch, wv.shape[1], s), BF16),
                   jax.ShapeDtypeStruct((t, ws.shape[1]), F32)],
        scratch_shapes=[pltpu.VMEM((SUBLANES, conv_w.shape[1]), F32)],
        compiler_params=pltpu.CompilerParams(dimension_semantics=("arbitrary",),
                                             vmem_limit_bytes=VMEM_LIMIT),
        name="inproj",
    )(x2d, wg, wq, wk, wv, ws, conv_w)


def _gdn_kernel(q_ref, k_ref, v_ref, z_ref, s_ref, ab_ref, nw_ref, o_ref,
                state_ref, us_ref, wq_ref, ai_ref, kt_ref, gl_ref, *, ts):
    c_len = GDN_CHUNK
    n_chunks = ts // c_len
    nh = GDN_HEADS
    n = nh * n_chunks
    i = pl.program_id(1)
    w_slot = i % 2
    r_slot = 1 - w_slot

    @pl.when(i == 0)
    def _():
        state_ref[...] = jnp.zeros_like(state_ref)
        us_ref[...] = jnp.zeros_like(us_ref)
        wq_ref[...] = jnp.zeros_like(wq_ref)
        ai_ref[...] = jnp.zeros_like(ai_ref)
        kt_ref[...] = jnp.zeros_like(kt_ref)
        gl_ref[...] = jnp.zeros_like(gl_ref)

    nw = nw_ref[...]
    rec = {"states": [state_ref[h] for h in range(nh)], "outs": [[None] * n_chunks for _ in range(nh)]}
    r_wq = [wq_ref[r_slot, b] for b in range(n)]
    r_us = [us_ref[r_slot, b] for b in range(n)]
    r_ai = [ai_ref[r_slot, b] for b in range(n)]
    r_kt = [kt_ref[r_slot, b] for b in range(n)]
    r_gl = [gl_ref[r_slot, b] for b in range(n)]

    def rec_read_state(cc):
        idx = [h * n_chunks + cc for h in range(nh)]
        sb = [st.astype(BF16) for st in rec["states"]]
        ws_qs = [_dot(r_wq[idx[h]], sb[h]) for h in range(nh)]
        v_new = [r_us[idx[h]] - ws_qs[h][:c_len] for h in range(nh)]
        rec["ws_qs"] = ws_qs
        rec["vnb"] = [v.astype(BF16) for v in v_new]

    def rec_write_state(cc):
        idx = [h * n_chunks + cc for h in range(nh)]
        for h in range(nh):
            rec["outs"][h][cc] = rec["ws_qs"][h][c_len:] + _dot(r_ai[idx[h]], rec["vnb"][h])
        rec["states"] = [rec["states"][h] * r_gl[idx[h]] + _dot_tn(r_kt[idx[h]], rec["vnb"][h])
                         for h in range(nh)]

    def rec_finish():
        for h in range(nh):
            lo, hi = h * GDN_DK, (h + 1) * GDN_DK
            state_ref[h] = rec["states"][h]
            o = jnp.concatenate(rec["outs"][h], axis=0)
            o = o * lax.rsqrt(jnp.mean(o * o, axis=-1, keepdims=True) + RMS_EPS) * nw
            z = z_ref[0, :, lo:hi].astype(F32)
            o_ref[0, :, lo:hi] = (o * (z * _sigmoid(z))).astype(o_ref.dtype)

    rec_stages = []
    for cc in range(n_chunks):
        rec_stages += [functools.partial(rec_read_state, cc), functools.partial(rec_write_state, cc)]
    rec_stages.append(rec_finish)

    def weave():
        if rec_stages:
            rec_stages.pop(0)()

    qa = q_ref[0].astype(F32)
    ka = k_ref[0].astype(F32)
    va = v_ref[0].astype(F32)

    sg = s_ref[0]
    beta_all = _sigmoid(sg)
    sp_in = sg + ab_ref[1:2, :]
    softplus = jnp.maximum(sp_in, 0.0) + jnp.log(1.0 + jnp.exp(-jnp.abs(sp_in)))
    g_step = ab_ref[0:1, :] * softplus

    r = lax.broadcasted_iota(I32, (ts, ts), 0)
    c = lax.broadcasted_iota(I32, (ts, ts), 1)
    tri = jnp.where((r // c_len) == (c // c_len), jnp.where(c <= r, 1.0, 0.0), 0.0)
    gc = jnp.dot(tri, g_step, precision=HIGHEST, preferred_element_type=F32)
    gct = [gc[cc * c_len:(cc + 1) * c_len, :].T for cc in range(n_chunks)]

    ri = lax.broadcasted_iota(I32, (c_len, c_len), 0)
    ci = lax.broadcasted_iota(I32, (c_len, c_len), 1)
    causal = ci <= ri
    strict = ci < ri

    qs, ks, kbs, vbs, kbes, decays, qds = [], [], [], [], [], [], []
    for h in range(nh):
        lo, hi = h * GDN_DK, (h + 1) * GDN_DK
        qh = qa[:, lo:hi]
        kh = ka[:, lo:hi]
        vh = va[:, lo:hi]
        beta = beta_all[:, h:h + 1]
        gcol_all = gc[:, nh + h:nh + h + 1]
        for cc in range(n_chunks):
            b = h * n_chunks + cc
            r0, r1 = cc * c_len, (cc + 1) * c_len
            qc, kc, vc = qh[r0:r1], kh[r0:r1], vh[r0:r1]
            bcol = beta[r0:r1]
            gcol = gcol_all[r0:r1]
            grow = gct[cc][nh + h:nh + h + 1, :]
            decay = jnp.where(causal, jnp.exp(jnp.where(causal, gcol - grow, 0.0)), 0.0)
            eg = jnp.exp(gcol)
            g_last = gcol[c_len - 1:c_len, :]
            kb = kc * bcol
            qs.append(qc.astype(BF16))
            ks.append(kc.astype(BF16))
            kbs.append(kb)
            vbs.append(vc * bcol)
            kbes.append(kb * eg)
            decays.append(decay)
            qds.append((qc * eg).astype(BF16))
            kt_ref[w_slot, b] = (kc * jnp.exp(g_last - gcol)).astype(BF16)
            gl_ref[w_slot, b] = jnp.broadcast_to(jnp.exp(g_last), (1, GDN_DV))
        weave()

    kq = [_dot_nt(jnp.concatenate([kbs[b].astype(BF16), qs[b]], axis=0), ks[b]) for b in range(n)]
    ms = [-jnp.where(strict, kq[b][:c_len] * decays[b], 0.0) for b in range(n)]
    for b in range(n):
        ai_ref[w_slot, b] = jnp.where(causal, kq[b][c_len:] * decays[b], 0.0).astype(BF16)
    weave()
    ys = ms
    for _ in range(5):
        mb = [m.astype(BF16) for m in ms]
        ms = [_dot(mb[b], mb[b]) for b in range(n)]
        weave()
        mb = [m.astype(BF16) for m in ms]
        ys = [ys[b] + ms[b] + _dot(ys[b].astype(BF16), mb[b]) for b in range(n)]
        weave()
    rhs = [jnp.concatenate([vbs[b], kbes[b]], axis=1) for b in range(n)]
    for b in range(n):
        uw = rhs[b] + _dot(ys[b].astype(BF16), rhs[b].astype(BF16))
        us_ref[w_slot, b] = uw[:, :GDN_DV]
        wq_ref[w_slot, b] = jnp.concatenate([uw[:, GDN_DV:].astype(BF16), qds[b]], axis=0)
    while rec_stages:
        weave()


def _gdn(g3, s3, ab, nw, ts):
    b, s, _ = g3.shape
    n_t = s // ts
    n = GDN_HEADS * (ts // GDN_CHUNK)
    cur = lambda i: jnp.minimum(i, n_t - 1)
    prev = lambda i: jnp.maximum(i - 1, 0)

    return pl.pallas_call(
        functools.partial(_gdn_kernel, ts=ts),
        grid=(b, n_t + 1),
        in_specs=[pl.BlockSpec((1, ts, GDN_W), lambda bb, i: (bb, cur(i), 0)),
                  pl.BlockSpec((1, ts, GDN_W), lambda bb, i: (bb, cur(i), 1)),
                  pl.BlockSpec((1, ts, GDN_W), lambda bb, i: (bb, cur(i), 2)),
                  pl.BlockSpec((1, ts, GDN_W), lambda bb, i: (bb, prev(i), 3)),
                  pl.BlockSpec((1, ts, SMALL_W), lambda bb, i: (bb, cur(i), 0)),
                  pl.BlockSpec(ab.shape, lambda bb, i: (0, 0)),
                  pl.BlockSpec(nw.shape, lambda bb, i: (0, 0))],
        out_specs=pl.BlockSpec((1, ts, GDN_W), lambda bb, i: (bb, prev(i), 0)),
        out_shape=jax.ShapeDtypeStruct((b, s, GDN_W), BF16),
        scratch_shapes=[pltpu.VMEM((GDN_HEADS, GDN_DK, GDN_DV), F32),
                        pltpu.VMEM((2, n, GDN_CHUNK, GDN_DV), F32),
                        pltpu.VMEM((2, n, 2 * GDN_CHUNK, GDN_DV), BF16),
                        pltpu.VMEM((2, n, GDN_CHUNK, GDN_CHUNK), BF16),
                        pltpu.VMEM((2, n, GDN_CHUNK, GDN_DK), BF16),
                        pltpu.VMEM((2, n, 1, GDN_DV), F32)],
        compiler_params=pltpu.CompilerParams(dimension_semantics=("parallel", "arbitrary"),
                                             vmem_limit_bytes=VMEM_LIMIT),
        name="gdn",
    )(g3, g3, g3, g3, s3, ab, nw)


def _diff_kernel(lam_ref, qt_ref, k_ref, vt_ref, nw_ref, o_ref, m_ref, acc_ref, s_ref, cm_ref, *, tq, tk,
                 out_scale):
    i = pl.program_id(2)
    heads = range(m_ref.shape[0])
    dq2 = 2 * DIFF_DQK
    rowi = lax.broadcasted_iota(I32, (dq2, tq), 0)
    qq = []
    for hd in heads:
        qt = qt_ref[0, hd * dq2:(hd + 1) * dq2, :]
        zero = jnp.zeros_like(qt)
        qq.append(jnp.concatenate([jnp.where(rowi < DIFF_DQK, qt, zero),
                                   jnp.where(rowi >= DIFF_DQK, qt, zero)], axis=1))
    m_ref[...] = jnp.full(m_ref.shape, NEG_BIG, F32)
    acc_ref[...] = jnp.zeros(acc_ref.shape, F32)
    ones_rows = jnp.ones((BF16_SUBLANES, tk), BF16)
    strip = DIFF_LANE_STRIP

    def scores(key_block, slot, diag):
        start = pl.multiple_of(key_block * tk, tk)
        for hd in heads:
            kj = k_ref[0, pl.ds(start, tk), hd * dq2:(hd + 1) * dq2]
            for c0 in range(0, 2 * tq, strip):
                s = _dot(kj, qq[hd][:, c0:c0 + strip])
                if diag is not None:
                    kr = lax.broadcasted_iota(I32, (tk, strip), 0) + diag * tk
                    qc = lax.broadcasted_iota(I32, (tk, strip), 1) + c0 % tq
                    s = jnp.where(kr <= qc, s, NEG_BIG)
                s_ref[hd, slot, :, c0:c0 + strip] = s
                cm_ref[hd, slot, :, c0:c0 + strip] = jnp.max(s, axis=0, keepdims=True)

    def update(key_block, slot):
        start = pl.multiple_of(key_block * tk, tk)
        for hd in heads:
            vtj = vt_ref[0, hd * DIFF_DV:(hd + 1) * DIFF_DV, pl.ds(start, tk)]
            vt_ext = jnp.concatenate([vtj, ones_rows], axis=0)
            for c0 in range(0, 2 * tq, strip):
                m_old = m_ref[hd, :, c0:c0 + strip]
                m_new = jnp.maximum(m_old, cm_ref[hd, slot, :, c0:c0 + strip])
                alpha = jnp.exp2(m_old - m_new)
                p = jnp.exp2(s_ref[hd, slot, :, c0:c0 + strip] - m_new).astype(BF16)
                acc_ref[hd, :, c0:c0 + strip] = alpha * acc_ref[hd, :, c0:c0 + strip] + _dot(vt_ext, p)
                m_ref[hd, :, c0:c0 + strip] = m_new

    base = 2 * i
    scores(base, 0, 0)
    scores(base + 1, 1, 1)
    update(base, 0)

    def pair(jj, carry):
        prev = jnp.where(jj == 0, base + 1, 2 * jj - 1)
        scores(2 * jj, 0, None)
        update(prev, 1)
        scores(2 * jj + 1, 1, None)
        update(2 * jj, 0)
        return carry

    def two_pairs(jq, carry):
        pair(2 * jq, carry)
        pair(2 * jq + 1, carry)
        return carry

    lax.fori_loop(0, i // 2, two_pairs, 0)

    @pl.when(i % 2 == 1)
    def _():
        pair(i - 1, 0)

    update(jnp.where(i == 0, base + 1, base - 1), 1)

    for hd in heads:
        acc = acc_ref[hd]
        o = acc[:DIFF_DV] / acc[DIFF_DV:DIFF_DV + 1]
        od = o[:, :tq] - lam_ref[0] * o[:, tq:]
        od = od * lax.rsqrt(jnp.mean(od * od, axis=0, keepdims=True) + RMS_EPS) * nw_ref[...] * out_scale
        o_ref[0, :, hd * DIFF_DV:(hd + 1) * DIFF_DV] = od.T.astype(o_ref.dtype)


def _diff_attn(lam, qt, k3, vt, nw_col, tq, out_scale):
    b, s, _ = k3.shape
    nh = DIFF_HEADS
    hp = DIFF_HEADS_PER_STEP
    tk = tq // 2
    wq, wv = hp * 2 * DIFF_DQK, hp * DIFF_DV
    return pl.pallas_call(
        functools.partial(_diff_kernel, tq=tq, tk=tk, out_scale=out_scale),
        grid=(b, nh // hp, s // tq),
        in_specs=[pl.BlockSpec(memory_space=pltpu.SMEM),
                  pl.BlockSpec((1, wq, tq), lambda bb, h, i: (bb, h, i)),
                  pl.BlockSpec((1, s, wq), lambda bb, h, i: (bb, 0, h)),
                  pl.BlockSpec((1, wv, s), lambda bb, h, i: (bb, h, 0)),
                  pl.BlockSpec(nw_col.shape, lambda bb, h, i: (0, 0))],
        out_specs=pl.BlockSpec((1, tq, wv), lambda bb, h, i: (bb, i, h)),
        out_shape=jax.ShapeDtypeStruct((b, s, nh * DIFF_DV), BF16),
        scratch_shapes=[pltpu.VMEM((hp, 1, 2 * tq), F32),
                        pltpu.VMEM((hp, DIFF_DV + BF16_SUBLANES, 2 * tq), F32),
                        pltpu.VMEM((hp, 2, tk, 2 * tq), F32), pltpu.VMEM((hp, 2, 1, 2 * tq), F32)],
        compiler_params=pltpu.CompilerParams(dimension_semantics=("parallel", "parallel", "arbitrary"),
                                             vmem_limit_bytes=VMEM_LIMIT),
        name="diff_attn",
    )(lam, qt, k3, vt, nw_col)


def _memkv_kernel(mem_ref, wk_ref, wv_ref, kt_ref, v_ref, *, scale):
    mb = mem_ref[0].astype(BF16)
    k = _dot(mb, wk_ref[...])
    kt_ref[0] = (k.T * scale).astype(BF16)
    v_ref[0] = _dot(mb, wv_ref[...]).astype(BF16)


def _memkv(mem, wk, wv, scale):
    b, m, d = mem.shape
    return pl.pallas_call(
        functools.partial(_memkv_kernel, scale=scale),
        grid=(b,),
        in_specs=[pl.BlockSpec((1, m, d), lambda bb: (bb, 0, 0)),
                  pl.BlockSpec(wk.shape, lambda bb: (0, 0)),
                  pl.BlockSpec(wv.shape, lambda bb: (0, 0))],
        out_specs=[pl.BlockSpec((1, d, m), lambda bb: (bb, 0, 0)),
                   pl.BlockSpec((1, m, d), lambda bb: (bb, 0, 0))],
        out_shape=[jax.ShapeDtypeStruct((b, d, m), BF16), jax.ShapeDtypeStruct((b, m, d), BF16)],
        compiler_params=pltpu.CompilerParams(dimension_semantics=("parallel",),
                                             vmem_limit_bytes=VMEM_LIMIT),
        name="memkv",
    )(mem, wk, wv)


def _mid_kernel(x_ref, og_ref, od_ref, wo1_ref, wo2_ref, g1_ref, b1_ref, wq_ref, kt_ref, v_ref, wo_ref,
                g2_ref, b2_ref, rw_ref, rw1_ref, rb_ref, x2_ref, x2p_ref, route_ref, gate_ref, cnt_ref, run_ref):
    i = pl.program_id(0)

    @pl.when(i == 0)
    def _():
        run_ref[...] = jnp.zeros_like(run_ref)

    tm, d = x_ref.shape
    sub = tm // MID_SUBBLOCKS
    spans = [(k * sub, (k + 1) * sub) for k in range(MID_SUBBLOCKS)]
    dh = d // XA_HEADS

    h = [_dot(og_ref[a:b, :], wo1_ref[...]) + _dot(od_ref[a:b, :], wo2_ref[...]) for a, b in spans]
    x1 = [_layer_norm(DEEPNORM_ALPHA * x_ref[a:b, :] + hk, g1_ref[...], b1_ref[...])
          for (a, b), hk in zip(spans, h)]
    q = [_dot(xk.astype(BF16), wq_ref[...]).astype(BF16) for xk in x1]
    heads = [[] for _ in spans]
    for hh in range(XA_HEADS):
        lo, hi = hh * dh, (hh + 1) * dh
        s = [_dot(qk[:, lo:hi], kt_ref[0, lo:hi, :]) for qk in q]
        p = [jnp.exp(sk - jnp.max(sk, axis=-1, keepdims=True)) for sk in s]
        p = [(pk / jnp.sum(pk, axis=-1, keepdims=True)).astype(BF16) for pk in p]
        for k, pk in enumerate(p):
            heads[k].append(_dot(pk, v_ref[0, :, lo:hi]))
    o = [jnp.concatenate(hk, axis=-1).astype(BF16) for hk in heads]
    h2 = [_dot(ok, wo_ref[...]) for ok in o]
    x2 = [_layer_norm(DEEPNORM_ALPHA * xk + hk, g2_ref[...], b2_ref[...]) for xk, hk in zip(x1, h2)]

    x_hi = [xk.astype(BF16) for xk in x2]
    x_lo = [(xk - hk.astype(F32)).astype(BF16) for xk, hk in zip(x2, x_hi)]
    r1 = [_dot(hk, rw_ref[...]) for hk in x_hi]
    r2 = [_dot(lk, rw1_ref[...]) for lk in x_lo]
    logits = [a1 + pltpu.roll(a1, LANES - N_EXPERTS, 1) + a2 + rb_ref[...] for a1, a2 in zip(r1, r2)]

    lane = lax.broadcasted_iota(I32, (sub, LANES), 1)
    rr = lax.broadcasted_iota(I32, (sub, sub), 0)
    cc = lax.broadcasted_iota(I32, (sub, sub), 1)
    tri = jnp.where(cc < rr, 1.0, 0.0).astype(BF16)
    run = run_ref[...]
    for (a, b), xk, lgk in zip(spans, x2, logits):
        x2_ref[a:b, :] = xk
        packed = _pack_bf16_pairs(xk)
        for piece in range(x2p_ref.shape[0]):
            x2p_ref[piece, a:b, :] = packed[:, piece * SC_ROW_WORDS:(piece + 1) * SC_ROW_WORDS]
        lg = jnp.where(lane < N_EXPERTS, lgk, NEG_BIG)
        sel = jnp.zeros((sub, LANES), F32)
        vals, hots, idxs = [], [], []
        for _ in range(TOP_K):
            mx = jnp.max(lg, axis=-1, keepdims=True)
            idx = jnp.min(jnp.where(lg == mx, lane, LANES), axis=-1, keepdims=True)
            hot = lane == idx
            vals.append(mx)
            idxs.append(idx)
            hots.append(hot)
            lg = jnp.where(hot, NEG_BIG, lg)
            sel = sel + jnp.where(hot, 1.0, 0.0)
        ex = [jnp.exp(v - vals[0]) for v in vals]
        den = ex[0] + ex[1] + ex[2] + ex[3]
        before = _dot(tri, sel.astype(BF16)) + run
        route = jnp.zeros((sub, LANES), F32)
        gates = jnp.zeros((sub, LANES), F32)
        for kk in range(TOP_K):
            rank = jnp.sum(jnp.where(hots[kk], before, 0.0), axis=-1, keepdims=True)
            route = jnp.where(lane == kk, idxs[kk].astype(F32), route)
            route = jnp.where(lane == TOP_K + kk, rank, route)
            gates = jnp.where(lane == kk, ex[kk] / den, gates)
        route_ref[:, a:b] = route.T[:2 * TOP_K].astype(I32)
        gate_ref[a:b, :] = gates
        run = run + jnp.sum(sel, axis=0, keepdims=True)
    run_ref[...] = run
    cnt_ref[...] = run.astype(I32)


def _mid(x2d, og, od, wo1, wo2, g1, b1, wq, kt, v, wo, g2, b2, rw, rw1, rb, tm, rows_per_batch):
    t, d = x2d.shape
    blocks_per_batch = rows_per_batch // tm
    pieces = (d // 2) // SC_ROW_WORDS
    row = lambda w: pl.BlockSpec((tm, w), lambda i: (i, 0))
    full = lambda a: pl.BlockSpec(a.shape, lambda i: (0, 0))
    per_batch = lambda a: pl.BlockSpec((1,) + a.shape[1:], lambda i: (i // blocks_per_batch, 0, 0))
    return pl.pallas_call(
        _mid_kernel,
        grid=(t // tm,),
        in_specs=[row(d), row(og.shape[1]), row(od.shape[1]), full(wo1), full(wo2), full(g1), full(b1),
                  full(wq), per_batch(kt), per_batch(v), full(wo), full(g2), full(b2), full(rw), full(rw1), full(rb)],
        out_specs=[row(d), pl.BlockSpec((pieces, tm, SC_ROW_WORDS), lambda i: (0, i, 0)),
                   pl.BlockSpec((2 * TOP_K, tm), lambda i: (0, i)), row(LANES),
                   pl.BlockSpec((1, LANES), lambda i: (0, 0))],
        out_shape=[jax.ShapeDtypeStruct((t, d), F32), jax.ShapeDtypeStruct((pieces, t, SC_ROW_WORDS), U32),
                   jax.ShapeDtypeStruct((2 * TOP_K, t), I32), jax.ShapeDtypeStruct((t, LANES), F32),
                   jax.ShapeDtypeStruct((1, LANES), I32)],
        scratch_shapes=[pltpu.VMEM((1, LANES), F32)],
        compiler_params=pltpu.CompilerParams(dimension_semantics=("arbitrary",),
                                             vmem_limit_bytes=VMEM_LIMIT),
        name="mid",
    )(x2d, og, od, wo1, wo2, g1, b1, wq, kt, v, wo, g2, b2, rw, rw1, rb)


def _sc_scatter_rows(rows, idx4, n_out):
    t, d = rows.shape
    mesh = plsc.VectorSubcoreMesh(core_axis_name="c", subcore_axis_name="s")

    @functools.partial(pl.kernel, out_type=jax.ShapeDtypeStruct((n_out, d), rows.dtype), mesh=mesh)
    def scatter_kernel(x_hbm, i0_hbm, i1_hbm, i2_hbm, i3_hbm, o_hbm):
        def body(x_vmem, i0, i1, i2, i3):
            for iv in (i0, i1, i2, i3):
                pltpu.sync_copy(x_vmem, o_hbm.at[iv.at[0]])

        ispec = pl.BlockSpec((1, SC_WINDOW), lambda i: (0, i))
        pltpu.emit_pipeline(
            body,
            grid=(t // SC_WINDOW,),
            in_specs=[pl.BlockSpec((SC_WINDOW, d), lambda i: (i, 0)), ispec, ispec, ispec, ispec],
            out_specs=[],
            core_axis_name=("c", "s"),
            dimension_semantics=(pltpu.PARALLEL,),
        )(x_hbm, i0_hbm, i1_hbm, i2_hbm, i3_hbm)

    return scatter_kernel(rows, *[idx4[k:k + 1] for k in range(TOP_K)])


def _sc_gather_rows(table, idx):
    n = idx.shape[0]
    d = table.shape[1]
    mesh = plsc.VectorSubcoreMesh(core_axis_name="c", subcore_axis_name="s")

    @functools.partial(pl.kernel, out_type=jax.ShapeDtypeStruct((n, d), table.dtype), mesh=mesh)
    def gather_kernel(x_hbm, i_hbm, o_hbm):
        def body(i_vmem, o_vmem):
            pltpu.sync_copy(x_hbm.at[i_vmem.at[0]], o_vmem)

        pltpu.emit_pipeline(
            body,
            grid=(n // SC_WINDOW,),
            in_specs=[pl.BlockSpec((1, SC_WINDOW), lambda i: (0, i))],
            out_specs=[pl.BlockSpec((SC_WINDOW, d), lambda i: (i, 0))],
            core_axis_name=("c", "s"),
            dimension_semantics=(pltpu.PARALLEL,),
        )(i_hbm, o_hbm)

    return gather_kernel(table, idx.reshape(1, n))


def _moe_kernel(be_ref, nu_ref, nx_ref, x_ref, wg_hbm, bg_ref, wu_hbm, bu_ref, wd_hbm, bd_ref, y_ref,
                stage_ref, slot_ref, sem):
    i = pl.program_id(0)
    used = i < nu_ref[0]
    expert = be_ref[i]
    prev = be_ref[jnp.maximum(i - 1, 0)]
    changed = jnp.logical_and(used, jnp.logical_or(i == 0, expert != prev))

    def weight_copies(e, slot):
        return [pltpu.make_async_copy(w.at[e], stage_ref.at[slot, k], sem.at[slot, k])
                for k, w in enumerate((wg_hbm, wu_hbm, wd_hbm))]

    @pl.when(changed)
    def _():
        @pl.when(i == 0)
        def _():
            slot_ref[0] = 1
            for cp in weight_copies(expert, 0):
                cp.start()

        slot = 1 - slot_ref[0]
        slot_ref[0] = slot
        for cp in weight_copies(expert, slot):
            cp.wait()
        nxt = nx_ref[i]

        @pl.when(nxt >= 0)
        def _():
            for cp in weight_copies(nxt, 1 - slot):
                cp.start(priority=WEIGHT_DMA_PRIORITY)

    @pl.when(used)
    def _():
        slot = slot_ref[0]
        halves = [_unpack_bf16_pairs(x_ref[piece]) for piece in range(x_ref.shape[0])]
        x = jnp.concatenate([h[0] for h in halves] + [h[1] for h in halves], axis=-1).astype(BF16)
        hg = _dot(x, stage_ref[slot, 0].astype(BF16)) + bg_ref[0]
        hl = _dot(x, stage_ref[slot, 1].astype(BF16)) + bu_ref[0]
        hg = jnp.minimum(hg, SWIGLU_LIMIT)
        hl = jnp.clip(hl, -SWIGLU_LIMIT, SWIGLU_LIMIT)
        act = hg * _sigmoid(SWIGLU_ALPHA * hg) * (hl + 1.0)
        y = _dot(act.astype(BF16), stage_ref[slot, 2].astype(BF16)) + bd_ref[0]
        packed = _pack_bf16_pairs(y)
        for piece in range(y_ref.shape[0]):
            y_ref[piece] = packed[:, piece * SC_ROW_WORDS:(piece + 1) * SC_ROW_WORDS]

    @pl.when(jnp.logical_not(used))
    def _():
        y_ref[...] = jnp.zeros_like(y_ref)


def _moe(block_e, n_used, next_e, xs, wg, bg, wu, bu, wd, bd, tm):
    pieces, p, dp = xs.shape
    d, dff = wg.shape[1], wg.shape[2]
    assert d == dff, "the shared f32 staging buffer assumes square expert matrices"
    n_blocks = p // tm
    bspec = lambda a: pl.BlockSpec((1,) + a.shape[1:], lambda i, be, nu, nx: (be[i], 0, 0))
    hbm = pl.BlockSpec(memory_space=pl.ANY)
    rows = pl.BlockSpec((pieces, tm, dp), lambda i, be, nu, nx: (0, i, 0))
    grid_spec = pltpu.PrefetchScalarGridSpec(
        num_scalar_prefetch=3,
        grid=(n_blocks,),
        in_specs=[rows, hbm, bspec(bg), hbm, bspec(bu), hbm, bspec(bd)],
        out_specs=rows,
        scratch_shapes=[pltpu.VMEM((2, 3, d, dff), F32), pltpu.SMEM((1,), I32),
                        pltpu.SemaphoreType.DMA((2, 3))],
    )
    return pl.pallas_call(
        _moe_kernel,
        grid_spec=grid_spec,
        out_shape=jax.ShapeDtypeStruct((pieces, p, dp), U32),
        compiler_params=pltpu.CompilerParams(dimension_semantics=("arbitrary",),
                                             vmem_limit_bytes=VMEM_LIMIT),
        name="moe_ffn",
    )(block_e, n_used, next_e, xs, wg, bg, wu, bu, wd, bd)


def _final_kernel(x_ref, y4_ref, gate_ref, g_ref, b_ref, o_ref):
    pieces = y4_ref.shape[0]
    gates = gate_ref[...]
    lows = [0.0] * pieces
    highs = [0.0] * pieces
    for kk in range(TOP_K):
        gk = gates[:, kk:kk + 1]
        for piece in range(pieces):
            lo, hi = _unpack_bf16_pairs(y4_ref[piece, kk])
            lows[piece] = lows[piece] + gk * lo
            highs[piece] = highs[piece] + gk * hi
    y = jnp.concatenate(lows + highs, axis=-1)
    o_ref[...] = _layer_norm(DEEPNORM_ALPHA * x_ref[...] + y, g_ref[...], b_ref[...])


def _final(x2, y4, gates, g, b, tm):
    t, d = x2.shape
    row = lambda w: pl.BlockSpec((tm, w), lambda i: (i, 0))
    full = lambda a: pl.BlockSpec(a.shape, lambda i: (0, 0))
    return pl.pallas_call(
        _final_kernel,
        grid=(t // tm,),
        in_specs=[row(d), pl.BlockSpec(y4.shape[:2] + (tm, y4.shape[3]), lambda i: (0, 0, i, 0)),
                  row(gates.shape[1]), full(g), full(b)],
        out_specs=row(d),
        out_shape=jax.ShapeDtypeStruct((t, d), F32),
        compiler_params=pltpu.CompilerParams(dimension_semantics=("parallel",),
                                             vmem_limit_bytes=VMEM_LIMIT),
        name="final_ln",
    )(x2, y4, gates, g, b)


def _layout(route, counts, tm):
    t = route.shape[1]
    experts = route[:TOP_K]
    ranks = route[TOP_K:]
    counts = counts[0, :N_EXPERTS]
    padded = (counts + tm - 1) // tm * tm
    pad_end = jnp.cumsum(padded)
    pad_start = pad_end - padded
    eids = jnp.arange(N_EXPERTS, dtype=I32)[:, None, None]
    dest = ranks + jnp.sum(jnp.where(experts[None] == eids, pad_start[:, None, None], 0), axis=0)
    n_blocks = t * TOP_K // tm + N_EXPERTS
    starts = jnp.arange(n_blocks, dtype=I32) * tm
    block_e = jnp.minimum(jnp.sum((pad_end[None, :] <= starts[:, None]).astype(I32), axis=-1), N_EXPERTS - 1)
    n_used = (pad_end[-1] // tm).astype(I32).reshape(1)
    e_row = jnp.arange(N_EXPERTS, dtype=I32)
    later = (e_row[None, :] > e_row[:, None]) & (padded[None, :] > 0)
    next_of = jnp.min(jnp.where(later, e_row[None, :], N_EXPERTS), axis=1)
    next_of = jnp.where(next_of == N_EXPERTS, -1, next_of)
    next_e = jnp.sum(jnp.where(block_e[:, None] == e_row[None, :], next_of[None, :], 0), axis=1)
    return dest.astype(I32), block_e.astype(I32), n_used, next_e.astype(I32), n_blocks


def kernel(x, mem, w_in, gdn_conv_w, gdn_a_log, gdn_dt_bias, gdn_norm_w, diff_lq1, diff_lk1, diff_lq2, diff_lk2,
           diff_norm_w, w_out, ln1_g, ln1_b, xa_wq, xa_wk, xa_wv, xa_wo, ln2_g, ln2_b, router_w, router_b,
           exp_w_gate, exp_b_gate, exp_w_up, exp_b_up, exp_w_down, exp_b_down, ln3_g, ln3_b):
    b, s, d = x.shape
    t = b * s
    x2d = x.reshape(t, d)
    for l in range(DEPTH):
        lambda_init = 0.8 - 0.6 * math.exp(-0.3 * l)
        n_g = 4 * GDN_W
        w = w_in[l]
        wg = w[:, :n_g].astype(BF16)
        small = w[:, n_g:n_g + 2 * GDN_HEADS]
        ws = jnp.pad(small, ((0, 0), (0, SMALL_W - 2 * GDN_HEADS))).astype(BF16)
        d0 = n_g + 2 * GDN_HEADS
        wdq = w[:, d0:d0 + DIFF_W].astype(BF16)
        wdk = w[:, d0 + DIFF_W:d0 + 2 * DIFF_W].astype(BF16)
        wdv = w[:, d0 + 2 * DIFF_W:d0 + 3 * DIFF_W].astype(BF16)
        conv_w = gdn_conv_w[l].reshape(GDN_CONV, 3 * GDN_W).astype(F32)
        ab = jnp.zeros((2, SMALL_W), F32)
        ab = ab.at[0, GDN_HEADS:2 * GDN_HEADS].set(-jnp.exp(gdn_a_log[l].astype(F32)))
        ab = ab.at[1, GDN_HEADS:2 * GDN_HEADS].set(gdn_dt_bias[l].astype(F32))
        lam = (jnp.exp(jnp.sum(diff_lq1[l].astype(F32) * diff_lk1[l].astype(F32)))
               - jnp.exp(jnp.sum(diff_lq2[l].astype(F32) * diff_lk2[l].astype(F32))) + lambda_init).reshape(1)

        g2d, dqt, dk2d, dvt, s2d = _inproj(x2d, wg, wdq, wdk, wdv, ws, conv_w, tm=512, batch=b,
                                           q_scale=LOG2E * DIFF_DQK ** -0.5)
        o_gdn = _gdn(g2d.reshape(b, s, n_g), s2d.reshape(b, s, SMALL_W), ab,
                     gdn_norm_w[l].reshape(1, GDN_DV).astype(F32), ts=256)
        o_diff = _diff_attn(lam, dqt, dk2d.reshape(b, s, DIFF_W), dvt,
                            diff_norm_w[l].reshape(DIFF_DV, 1).astype(F32), tq=512, out_scale=1.0 - lambda_init)

        dh = d // XA_HEADS
        kt, v = _memkv(mem, xa_wk[l].astype(BF16), xa_wv[l].astype(BF16), scale=dh ** -0.5)
        wo_mix = w_out[l].astype(BF16)
        rw_f = router_w[l].astype(F32)
        rw_hi = rw_f.astype(BF16)
        rw_lo = (rw_f - rw_hi.astype(F32)).astype(BF16)
        rw = jnp.pad(jnp.concatenate([rw_hi, rw_lo], axis=1), ((0, 0), (0, LANES - 2 * N_EXPERTS)))
        rw1 = jnp.pad(rw_hi, ((0, 0), (0, LANES - N_EXPERTS)))
        rb = jnp.pad(router_b[l].astype(F32), (0, LANES - N_EXPERTS)).reshape(1, LANES)
        row = lambda a: a.reshape(1, -1).astype(F32)
        x2, x2p, route, gates, counts = _mid(
            x2d, o_gdn.reshape(t, GDN_W), o_diff.reshape(t, DIFF_HEADS * DIFF_DV),
            wo_mix[:GDN_W], wo_mix[GDN_W:], row(ln1_g[l]), row(ln1_b[l]),
            xa_wq[l].astype(BF16), kt, v, xa_wo[l].astype(BF16), row(ln2_g[l]), row(ln2_b[l]),
            rw, rw1, rb, tm=1024, rows_per_batch=s)

        tm_moe = 256
        dest, block_e, n_used, next_e, n_blocks = _layout(route, counts, tm_moe)
        pieces = x2p.shape[0]
        p_rows = n_blocks * tm_moe
        offs = (jnp.arange(pieces, dtype=I32) * p_rows)[:, None]
        idx_scatter = (dest[:, None, :] + offs[None]).reshape(TOP_K, pieces * t)
        xs = _sc_scatter_rows(x2p.reshape(pieces * t, SC_ROW_WORDS), idx_scatter,
                              pieces * p_rows).reshape(pieces, p_rows, SC_ROW_WORDS)
        ys = _moe(block_e, n_used, next_e, xs,
                  exp_w_gate[l], exp_b_gate[l].reshape(N_EXPERTS, 1, -1), exp_w_up[l],
                  exp_b_up[l].reshape(N_EXPERTS, 1, -1), exp_w_down[l], exp_b_down[l].reshape(N_EXPERTS, 1, -1),
                  tm=tm_moe)
        idx_gather = (dest.reshape(1, TOP_K * t) + offs).reshape(pieces * TOP_K * t)
        y4 = _sc_gather_rows(ys.reshape(pieces * p_rows, SC_ROW_WORDS),
                             idx_gather).reshape(pieces, TOP_K, t, SC_ROW_WORDS)
        x2d = _final(x2, y4, gates, row(ln3_g[l]), row(ln3_b[l]), tm=512)
    return x2d.reshape(b, s, d)
```

```python
import functools
import math

import jax
import jax.numpy as jnp
from jax import lax
from jax.experimental import pallas as pl
from jax.experimental.pallas import tpu as pltpu
from jax.experimental.pallas import tpu_sc as plsc

F32 = jnp.float32
BF16 = jnp.bfloat16
U32 = jnp.uint32
I32 = jnp.int32
HIGHEST = lax.Precision.HIGHEST

GDN_HEADS = 4
GDN_DK = 128
GDN_DV = 128
GDN_CONV = 4
GDN_CHUNK = 64
DIFF_HEADS = 4
DIFF_DQK = 64
DIFF_DV = 128
XA_HEADS = 4
N_EXPERTS = 32
TOP_K = 4
SWIGLU_ALPHA = 1.702
SWIGLU_LIMIT = 7.0
LN_EPS = 1e-5
RMS_EPS = 1e-6
DEPTH = 1
DEEPNORM_ALPHA = (2 * DEPTH) ** 0.25

GDN_W = GDN_HEADS * GDN_DK
DIFF_W = DIFF_HEADS * 2 * DIFF_DQK
SMALL_W = 128

LANES = 128
SUBLANES = 8
BF16_SUBLANES = 16
VMEM_LIMIT = 56 * 1024 * 1024
SC_WINDOW = 128
SC_ROW_WORDS = 256
DIFF_HEADS_PER_STEP = 2
DIFF_LANE_STRIP = 256
WEIGHT_DMA_PRIORITY = 1
MID_SUBBLOCKS = 4

INPROJ_ROWS = 512
GDN_ROWS = 256
DIFF_QUERY_ROWS = 512
MID_ROWS = 1024
MOE_ROWS = 256
FINAL_ROWS = 512

NEG_BIG = -1e30
LOG2E = 1.4426950408889634


def _dot(a, b):
    return jnp.dot(a, b, preferred_element_type=F32)


def _dot_nt(a, b):
    return lax.dot_general(a, b, (((1,), (1,)), ((), ())), preferred_element_type=F32)


def _dot_tn(a, b):
    return lax.dot_general(a, b, (((0,), (0,)), ((), ())), preferred_element_type=F32)


def _layer_norm(y, g, b):
    mu = jnp.mean(y, axis=-1, keepdims=True)
    d = y - mu
    var = jnp.mean(d * d, axis=-1, keepdims=True)
    return d * lax.rsqrt(var + LN_EPS) * g + b


def _sigmoid(x):
    return 1.0 / (1.0 + jnp.exp(-x))


def _pack_bf16_pairs(x):
    n = x.shape[1] // 2
    bits = pltpu.bitcast(x.astype(BF16).astype(F32), U32)
    return (bits[:, :n] >> 16) | bits[:, n:]


def _unpack_bf16_pairs(w):
    lo = pltpu.bitcast(w << 16, F32)
    hi = pltpu.bitcast(w & jnp.uint32(0xFFFF0000), F32)
    return lo, hi


def _inproj_kernel(x_ref, wg_ref, wq_ref, wk_ref, wv_ref, ws_ref, cw_ref, g_ref, qt_ref, k_ref, vt_ref, s_ref,
                   tail_ref, *, q_scale, per_batch):
    i = pl.program_id(0)
    tm = x_ref.shape[0]
    n_conv = 3 * GDN_W

    @pl.when(i == 0)
    def _():
        tail_ref[...] = jnp.zeros_like(tail_ref)

    xb = x_ref[...].astype(BF16)
    first = i % per_batch == 0
    slab = GDN_W
    row8 = lax.broadcasted_iota(I32, (SUBLANES, slab), 0)

    def proj_z_qt():
        g_ref[:, n_conv:] = _dot(xb, wg_ref[:, n_conv:]).astype(BF16)
        qt_ref[0] = (_dot(xb, wq_ref[...]) * q_scale).T.astype(BF16)

    def proj_k():
        k_ref[...] = _dot(xb, wk_ref[...]).astype(BF16)

    def proj_vt_s():
        vt_ref[0] = _dot(xb, wv_ref[...]).T.astype(BF16)
        s_ref[...] = _dot(xb, ws_ref[...])

    others = [proj_z_qt, proj_k, proj_vt_s]
    assert n_conv // slab == len(others)

    for c0 in range(0, n_conv, slab):
        pre = _dot(xb, wg_ref[:, c0:c0 + slab])
        others[c0 // slab]()
        halo = jnp.where(first, 0.0, tail_ref[:, c0:c0 + slab])
        tail_ref[:, c0:c0 + slab] = pre[tm - SUBLANES:, :]
        w = cw_ref[:, c0:c0 + slab]
        y = pre * w[GDN_CONV - 1:GDN_CONV, :]
        for j in range(1, GDN_CONV):
            xr = pltpu.roll(pre, j, 0)
            top = jnp.where(row8 < j, pltpu.roll(halo, j, 0), xr[0:SUBLANES])
            xr = jnp.concatenate([top, xr[SUBLANES:]], axis=0)
            y = y + xr * w[GDN_CONV - 1 - j:GDN_CONV - j, :]
        y = y * _sigmoid(y)
        for lo in range(0, slab, GDN_DK):
            yh = y[:, lo:lo + GDN_DK]
            if c0 < GDN_W:
                yh = yh * (lax.rsqrt(jnp.sum(yh * yh, axis=-1, keepdims=True) + RMS_EPS) * (GDN_DK ** -0.5))
            elif c0 < 2 * GDN_W:
                yh = yh * lax.rsqrt(jnp.sum(yh * yh, axis=-1, keepdims=True) + RMS_EPS)
            g_ref[:, c0 + lo:c0 + lo + GDN_DK] = yh.astype(BF16)


def _inproj(x2d, wg, wq, wk, wv, ws, conv_w, tm, batch, q_scale):
    t, d = x2d.shape
    s = t // batch
    per_batch = s // tm
    full = lambda a: pl.BlockSpec(a.shape, lambda i: (0, 0))
    row = lambda w: pl.BlockSpec((tm, w), lambda i: (i, 0))
    tr = lambda w: pl.BlockSpec((1, w, tm), lambda i: (i // per_batch, 0, i % per_batch))
    return pl.pallas_call(
        functools.partial(_inproj_kernel, q_scale=q_scale, per_batch=per_batch),
        grid=(t // tm,),
        in_specs=[row(d), full(wg), full(wq), full(wk), full(wv), full(ws), full(conv_w)],
        out_specs=[row(wg.shape[1]), tr(wq.shape[1]), row(wk.shape[1]), tr(wv.shape[1]), row(ws.shape[1])],
        out_shape=[jax.ShapeDtypeStruct((t, wg.shape[1]), BF16),
                   jax.ShapeDtypeStruct((batch, wq.shape[1], s), BF16),
                   jax.ShapeDtypeStruct((t, wk.shape[1]), BF16),
                   jax.ShapeDtypeStruct((batch, wv.shape[1], s), BF16),
                   jax.ShapeDtypeStruct((t, ws.shape[1]), F32)],
        scratch_shapes=[pltpu.VMEM((SUBLANES, conv_w.shape[1]), F32)],
        compiler_params=pltpu.CompilerParams(dimension_semantics=("arbitrary",),
                                             vmem_limit_bytes=VMEM_LIMIT),
        name="inproj",
    )(x2d, wg, wq, wk, wv, ws, conv_w)


def _gdn_kernel(q_ref, k_ref, v_ref, z_ref, s_ref, ab_ref, nw_ref, o_ref,
                state_ref, us_ref, wq_ref, ai_ref, kt_ref, gl_ref, *, ts):
    c_len = GDN_CHUNK
    n_chunks = ts // c_len
    nh = GDN_HEADS
    n = nh * n_chunks
    i = pl.program_id(1)
    w_slot = i % 2
    r_slot = 1 - w_slot

    @pl.when(i == 0)
    def _():
        state_ref[...] = jnp.zeros_like(state_ref)
        us_ref[...] = jnp.zeros_like(us_ref)
        wq_ref[...] = jnp.zeros_like(wq_ref)
        ai_ref[...] = jnp.zeros_like(ai_ref)
        kt_ref[...] = jnp.zeros_like(kt_ref)
        gl_ref[...] = jnp.zeros_like(gl_ref)

    nw = nw_ref[...]
    rec = {"states": [state_ref[h] for h in range(nh)], "outs": [[None] * n_chunks for _ in range(nh)]}
    r_wq = [wq_ref[r_slot, b] for b in range(n)]
    r_us = [us_ref[r_slot, b] for b in range(n)]
    r_ai = [ai_ref[r_slot, b] for b in range(n)]
    r_kt = [kt_ref[r_slot, b] for b in range(n)]
    r_gl = [gl_ref[r_slot, b] for b in range(n)]

    def rec_read_state(cc):
        idx = [h * n_chunks + cc for h in range(nh)]
        sb = [st.astype(BF16) for st in rec["states"]]
        ws_qs = [_dot(r_wq[idx[h]], sb[h]) for h in range(nh)]
        v_new = [r_us[idx[h]] - ws_qs[h][:c_len] for h in range(nh)]
        rec["ws_qs"] = ws_qs
        rec["vnb"] = [v.astype(BF16) for v in v_new]

    def rec_write_state(cc):
        idx = [h * n_chunks + cc for h in range(nh)]
        for h in range(nh):
            rec["outs"][h][cc] = rec["ws_qs"][h][c_len:] + _dot(r_ai[idx[h]], rec["vnb"][h])
        rec["states"] = [rec["states"][h] * r_gl[idx[h]] + _dot_tn(r_kt[idx[h]], rec["vnb"][h])
                         for h in range(nh)]

    def rec_finish():
        for h in range(nh):
            lo, hi = h * GDN_DK, (h + 1) * GDN_DK
            state_ref[h] = rec["states"][h]
            o = jnp.concatenate(rec["outs"][h], axis=0)
            o = o * lax.rsqrt(jnp.mean(o * o, axis=-1, keepdims=True) + RMS_EPS) * nw
            z = z_ref[0, :, lo:hi].astype(F32)
            o_ref[0, :, lo:hi] = (o * (z * _sigmoid(z))).astype(o_ref.dtype)

    rec_stages = []
    for cc in range(n_chunks):
        rec_stages += [functools.partial(rec_read_state, cc), functools.partial(rec_write_state, cc)]
    rec_stages.append(rec_finish)

    def weave():
        if rec_stages:
            rec_stages.pop(0)()

    qa = q_ref[0].astype(F32)
    ka = k_ref[0].astype(F32)
    va = v_ref[0].astype(F32)

    sg = s_ref[0]
    beta_all = _sigmoid(sg)
    sp_in = sg + ab_ref[1:2, :]
    softplus = jnp.maximum(sp_in, 0.0) + jnp.log(1.0 + jnp.exp(-jnp.abs(sp_in)))
    g_step = ab_ref[0:1, :] * softplus

    r = lax.broadcasted_iota(I32, (ts, ts), 0)
    c = lax.broadcasted_iota(I32, (ts, ts), 1)
    tri = jnp.where((r // c_len) == (c // c_len), jnp.where(c <= r, 1.0, 0.0), 0.0)
    gc = jnp.dot(tri, g_step, precision=HIGHEST, preferred_element_type=F32)
    gct = [gc[cc * c_len:(cc + 1) * c_len, :].T for cc in range(n_chunks)]

    ri = lax.broadcasted_iota(I32, (c_len, c_len), 0)
    ci = lax.broadcasted_iota(I32, (c_len, c_len), 1)
    causal = ci <= ri
    strict = ci < ri

    qs, ks, kbs, vbs, kbes, decays, qds = [], [], [], [], [], [], []
    for h in range(nh):
        lo, hi = h * GDN_DK, (h + 1) * GDN_DK
        qh = qa[:, lo:hi]
        kh = ka[:, lo:hi]
        vh = va[:, lo:hi]
        beta = beta_all[:, h:h + 1]
        gcol_all = gc[:, nh + h:nh + h + 1]
        for cc in range(n_chunks):
            b = h * n_chunks + cc
            r0, r1 = cc * c_len, (cc + 1) * c_len
            qc, kc, vc = qh[r0:r1], kh[r0:r1], vh[r0:r1]
            bcol = beta[r0:r1]
            gcol = gcol_all[r0:r1]
            grow = gct[cc][nh + h:nh + h + 1, :]
            decay = jnp.where(causal, jnp.exp(jnp.where(causal, gcol - grow, 0.0)), 0.0)
            eg = jnp.exp(gcol)
            g_last = gcol[c_len - 1:c_len, :]
            kb = kc * bcol
            qs.append(qc.astype(BF16))
            ks.append(kc.astype(BF16))
            kbs.append(kb)
            vbs.append(vc * bcol)
            kbes.append(kb * eg)
            decays.append(decay)
            qds.append((qc * eg).astype(BF16))
            kt_ref[w_slot, b] = (kc * jnp.exp(g_last - gcol)).astype(BF16)
            gl_ref[w_slot, b] = jnp.broadcast_to(jnp.exp(g_last), (1, GDN_DV))
        weave()

    kq = [_dot_nt(jnp.concatenate([kbs[b].astype(BF16), qs[b]], axis=0), ks[b]) for b in range(n)]
    ms = [-jnp.where(strict, kq[b][:c_len] * decays[b], 0.0) for b in range(n)]
    for b in range(n):
        ai_ref[w_slot, b] = jnp.where(causal, kq[b][c_len:] * decays[b], 0.0).astype(BF16)
    weave()
    ys = ms
    for _ in range(5):
        mb = [m.astype(BF16) for m in ms]
        ms = [_dot(mb[b], mb[b]) for b in range(n)]
        weave()
        mb = [m.astype(BF16) for m in ms]
        ys = [ys[b] + ms[b] + _dot(ys[b].astype(BF16), mb[b]) for b in range(n)]
        weave()
    rhs = [jnp.concatenate([vbs[b], kbes[b]], axis=1) for b in range(n)]
    for b in range(n):
        uw = rhs[b] + _dot(ys[b].astype(BF16), rhs[b].astype(BF16))
        us_ref[w_slot, b] = uw[:, :GDN_DV]
        wq_ref[w_slot, b] = jnp.concatenate([uw[:, GDN_DV:].astype(BF16), qds[b]], axis=0)
    while rec_stages:
        weave()


def _gdn(g3, s3, ab, nw, ts):
    b, s, _ = g3.shape
    n_t = s // ts
    n = GDN_HEADS * (ts // GDN_CHUNK)
    cur = lambda i: jnp.minimum(i, n_t - 1)
    prev = lambda i: jnp.maximum(i - 1, 0)

    return pl.pallas_call(
        functools.partial(_gdn_kernel, ts=ts),
        grid=(b, n_t + 1),
        in_specs=[pl.BlockSpec((1, ts, GDN_W), lambda bb, i: (bb, cur(i), 0)),
                  pl.BlockSpec((1, ts, GDN_W), lambda bb, i: (bb, cur(i), 1)),
                  pl.BlockSpec((1, ts, GDN_W), lambda bb, i: (bb, cur(i), 2)),
                  pl.BlockSpec((1, ts, GDN_W), lambda bb, i: (bb, prev(i), 3)),
                  pl.BlockSpec((1, ts, SMALL_W), lambda bb, i: (bb, cur(i), 0)),
                  pl.BlockSpec(ab.shape, lambda bb, i: (0, 0)),
                  pl.BlockSpec(nw.shape, lambda bb, i: (0, 0))],
        out_specs=pl.BlockSpec((1, ts, GDN_W), lambda bb, i: (bb, prev(i), 0)),
        out_shape=jax.ShapeDtypeStruct((b, s, GDN_W), BF16),
        scratch_shapes=[pltpu.VMEM((GDN_HEADS, GDN_DK, GDN_DV), F32),
                        pltpu.VMEM((2, n, GDN_CHUNK, GDN_DV), F32),
                        pltpu.VMEM((2, n, 2 * GDN_CHUNK, GDN_DV), BF16),
                        pltpu.VMEM((2, n, GDN_CHUNK, GDN_CHUNK), BF16),
                        pltpu.VMEM((2, n, GDN_CHUNK, GDN_DK), BF16),
                        pltpu.VMEM((2, n, 1, GDN_DV), F32)],
        compiler_params=pltpu.CompilerParams(dimension_semantics=("parallel", "arbitrary"),
                                             vmem_limit_bytes=VMEM_LIMIT),
        name="gdn",
    )(g3, g3, g3, g3, s3, ab, nw)


def _diff_kernel(lam_ref, qt_ref, k_ref, vt_ref, nw_ref, o_ref, m_ref, acc_ref, s_ref, cm_ref, *, tq, tk,
                 out_scale):
    i = pl.program_id(2)
    heads = range(m_ref.shape[0])
    dq2 = 2 * DIFF_DQK
    rowi = lax.broadcasted_iota(I32, (dq2, tq), 0)
    qq = []
    for hd in heads:
        qt = qt_ref[0, hd * dq2:(hd + 1) * dq2, :]
        zero = jnp.zeros_like(qt)
        qq.append(jnp.concatenate([jnp.where(rowi < DIFF_DQK, qt, zero),
                                   jnp.where(rowi >= DIFF_DQK, qt, zero)], axis=1))
    m_ref[...] = jnp.full(m_ref.shape, NEG_BIG, F32)
    acc_ref[...] = jnp.zeros(acc_ref.shape, F32)
    ones_rows = jnp.ones((BF16_SUBLANES, tk), BF16)
    strip = DIFF_LANE_STRIP

    def scores(key_block, slot, diag):
        start = pl.multiple_of(key_block * tk, tk)
        for hd in heads:
            kj = k_ref[0, pl.ds(start, tk), hd * dq2:(hd + 1) * dq2]
            for c0 in range(0, 2 * tq, strip):
                s = _dot(kj, qq[hd][:, c0:c0 + strip])
                if diag is not None:
                    kr = lax.broadcasted_iota(I32, (tk, strip), 0) + diag * tk
                    qc = lax.broadcasted_iota(I32, (tk, strip), 1) + c0 % tq
                    s = jnp.where(kr <= qc, s, NEG_BIG)
                s_ref[hd, slot, :, c0:c0 + strip] = s
                cm_ref[hd, slot, :, c0:c0 + strip] = jnp.max(s, axis=0, keepdims=True)

    def update(key_block, slot):
        start = pl.multiple_of(key_block * tk, tk)
        for hd in heads:
            vtj = vt_ref[0, hd * DIFF_DV:(hd + 1) * DIFF_DV, pl.ds(start, tk)]
            vt_ext = jnp.concatenate([vtj, ones_rows], axis=0)
            for c0 in range(0, 2 * tq, strip):
                m_old = m_ref[hd, :, c0:c0 + strip]
                m_new = jnp.maximum(m_old, cm_ref[hd, slot, :, c0:c0 + strip])
                alpha = jnp.exp2(m_old - m_new)
                p = jnp.exp2(s_ref[hd, slot, :, c0:c0 + strip] - m_new).astype(BF16)
                acc_ref[hd, :, c0:c0 + strip] = alpha * acc_ref[hd, :, c0:c0 + strip] + _dot(vt_ext, p)
                m_ref[hd, :, c0:c0 + strip] = m_new

    base = 2 * i
    scores(base, 0, 0)
    scores(base + 1, 1, 1)
    update(base, 0)

    def pair(jj, carry):
        prev = jnp.where(jj == 0, base + 1, 2 * jj - 1)
        scores(2 * jj, 0, None)
        update(prev, 1)
        scores(2 * jj + 1, 1, None)
        update(2 * jj, 0)
        return carry

    def two_pairs(jq, carry):
        pair(2 * jq, carry)
        pair(2 * jq + 1, carry)
        return carry

    lax.fori_loop(0, i // 2, two_pairs, 0)

    @pl.when(i % 2 == 1)
    def _():
        pair(i - 1, 0)

    update(jnp.where(i == 0, base + 1, base - 1), 1)

    for hd in heads:
        acc = acc_ref[hd]
        o = acc[:DIFF_DV] / acc[DIFF_DV:DIFF_DV + 1]
        od = o[:, :tq] - lam_ref[0] * o[:, tq:]
        od = od * lax.rsqrt(jnp.mean(od * od, axis=0, keepdims=True) + RMS_EPS) * nw_ref[...] * out_scale
        o_ref[0, :, hd * DIFF_DV:(hd + 1) * DIFF_DV] = od.T.astype(o_ref.dtype)


def _diff_attn(lam, qt, k3, vt, nw_col, tq, out_scale):
    b, s, _ = k3.shape
    nh = DIFF_HEADS
    hp = DIFF_HEADS_PER_STEP
    tk = tq // 2
    wq, wv = hp * 2 * DIFF_DQK, hp * DIFF_DV
    return pl.pallas_call(
        functools.partial(_diff_kernel, tq=tq, tk=tk, out_scale=out_scale),
        grid=(b, nh // hp, s // tq),
        in_specs=[pl.BlockSpec(memory_space=pltpu.SMEM),
                  pl.BlockSpec((1, wq, tq), lambda bb, h, i: (bb, h, i)),
                  pl.BlockSpec((1, s, wq), lambda bb, h, i: (bb, 0, h)),
                  pl.BlockSpec((1, wv, s), lambda bb, h, i: (bb, h, 0)),
                  pl.BlockSpec(nw_col.shape, lambda bb, h, i: (0, 0))],
        out_specs=pl.BlockSpec((1, tq, wv), lambda bb, h, i: (bb, i, h)),
        out_shape=jax.ShapeDtypeStruct((b, s, nh * DIFF_DV), BF16),
        scratch_shapes=[pltpu.VMEM((hp, 1, 2 * tq), F32),
                        pltpu.VMEM((hp, DIFF_DV + BF16_SUBLANES, 2 * tq), F32),
                        pltpu.VMEM((hp, 2, tk, 2 * tq), F32), pltpu.VMEM((hp, 2, 1, 2 * tq), F32)],
        compiler_params=pltpu.CompilerParams(dimension_semantics=("parallel", "parallel", "arbitrary"),
                                             vmem_limit_bytes=VMEM_LIMIT),
        name="diff_attn",
    )(lam, qt, k3, vt, nw_col)


def _memkv_kernel(mem_ref, wk_ref, wv_ref, kt_ref, v_ref, *, scale):
    mb = mem_ref[0].astype(BF16)
    k = _dot(mb, wk_ref[...])
    kt_ref[0] = (k.T * scale).astype(BF16)
    v_ref[0] = _dot(mb, wv_ref[...]).astype(BF16)


def _memkv(mem, wk, wv, scale):
    b, m, d = mem.shape
    return pl.pallas_call(
        functools.partial(_memkv_kernel, scale=scale),
        grid=(b,),
        in_specs=[pl.BlockSpec((1, m, d), lambda bb: (bb, 0, 0)),
                  pl.BlockSpec(wk.shape, lambda bb: (0, 0)),
                  pl.BlockSpec(wv.shape, lambda bb: (0, 0))],
        out_specs=[pl.BlockSpec((1, d, m), lambda bb: (bb, 0, 0)),
                   pl.BlockSpec((1, m, d), lambda bb: (bb, 0, 0))],
        out_shape=[jax.ShapeDtypeStruct((b, d, m), BF16), jax.ShapeDtypeStruct((b, m, d), BF16)],
        compiler_params=pltpu.CompilerParams(dimension_semantics=("parallel",),
                                             vmem_limit_bytes=VMEM_LIMIT),
        name="memkv",
    )(mem, wk, wv)


def _mid_kernel(x_ref, og_ref, od_ref, wo1_ref, wo2_ref, g1_ref, b1_ref, wq_ref, kt_ref, v_ref, wo_ref,
                g2_ref, b2_ref, rw_ref, rw1_ref, rb_ref, x2_ref, x2p_ref, route_ref, gate_ref, cnt_ref, run_ref):
    i = pl.program_id(0)

    @pl.when(i == 0)
    def _():
        run_ref[...] = jnp.zeros_like(run_ref)

    tm, d = x_ref.shape
    sub = tm // MID_SUBBLOCKS
    spans = [(k * sub, (k + 1) * sub) for k in range(MID_SUBBLOCKS)]
    dh = d // XA_HEADS

    h = [_dot(og_ref[a:b, :], wo1_ref[...]) + _dot(od_ref[a:b, :], wo2_ref[...]) for a, b in spans]
    x1 = [_layer_norm(DEEPNORM_ALPHA * x_ref[a:b, :] + hk, g1_ref[...], b1_ref[...])
          for (a, b), hk in zip(spans, h)]
    q = [_dot(xk.astype(BF16), wq_ref[...]).astype(BF16) for xk in x1]
    heads = [[] for _ in spans]
    for hh in range(XA_HEADS):
        lo, hi = hh * dh, (hh + 1) * dh
        s = [_dot(qk[:, lo:hi], kt_ref[0, lo:hi, :]) for qk in q]
        p = [jnp.exp(sk - jnp.max(sk, axis=-1, keepdims=True)) for sk in s]
        p = [(pk / jnp.sum(pk, axis=-1, keepdims=True)).astype(BF16) for pk in p]
        for k, pk in enumerate(p):
            heads[k].append(_dot(pk, v_ref[0, :, lo:hi]))
    o = [jnp.concatenate(hk, axis=-1).astype(BF16) for hk in heads]
    h2 = [_dot(ok, wo_ref[...]) for ok in o]
    x2 = [_layer_norm(DEEPNORM_ALPHA * xk + hk, g2_ref[...], b2_ref[...]) for xk, hk in zip(x1, h2)]

    x_hi = [xk.astype(BF16) for xk in x2]
    x_lo = [(xk - hk.astype(F32)).astype(BF16) for xk, hk in zip(x2, x_hi)]
    r1 = [_dot(hk, rw_ref[...]) for hk in x_hi]
    r2 = [_dot(lk, rw1_ref[...]) for lk in x_lo]
    logits = [a1 + pltpu.roll(a1, LANES - N_EXPERTS, 1) + a2 + rb_ref[...] for a1, a2 in zip(r1, r2)]

    lane = lax.broadcasted_iota(I32, (sub, LANES), 1)
    rr = lax.broadcasted_iota(I32, (sub, sub), 0)
    cc = lax.broadcasted_iota(I32, (sub, sub), 1)
    tri = jnp.where(cc < rr, 1.0, 0.0).astype(BF16)
    run = run_ref[...]
    for (a, b), xk, lgk in zip(spans, x2, logits):
        x2_ref[a:b, :] = xk
        packed = _pack_bf16_pairs(xk)
        for piece in range(x2p_ref.shape[0]):
            x2p_ref[piece, a:b, :] = packed[:, piece * SC_ROW_WORDS:(piece + 1) * SC_ROW_WORDS]
        lg = jnp.where(lane < N_EXPERTS, lgk, NEG_BIG)
        sel = jnp.zeros((sub, LANES), F32)
        vals, hots, idxs = [], [], []
        for _ in range(TOP_K):
            mx = jnp.max(lg, axis=-1, keepdims=True)
            idx = jnp.min(jnp.where(lg == mx, lane, LANES), axis=-1, keepdims=True)
            hot = lane == idx
            vals.append(mx)
            idxs.append(idx)
            hots.append(hot)
            lg = jnp.where(hot, NEG_BIG, lg)
            sel = sel + jnp.where(hot, 1.0, 0.0)
        ex = [jnp.exp(v - vals[0]) for v in vals]
        den = ex[0] + ex[1] + ex[2] + ex[3]
        before = _dot(tri, sel.astype(BF16)) + run
        route = jnp.zeros((sub, LANES), F32)
        gates = jnp.zeros((sub, LANES), F32)
        for kk in range(TOP_K):
            rank = jnp.sum(jnp.where(hots[kk], before, 0.0), axis=-1, keepdims=True)
            route = jnp.where(lane == kk, idxs[kk].astype(F32), route)
            route = jnp.where(lane == TOP_K + kk, rank, route)
            gates = jnp.where(lane == kk, ex[kk] / den, gates)
        route_ref[:, a:b] = route.T[:2 * TOP_K].astype(I32)
        gate_ref[a:b, :] = gates
        run = run + jnp.sum(sel, axis=0, keepdims=True)
    run_ref[...] = run
    cnt_ref[...] = run.astype(I32)


def _mid(x2d, og, od, wo1, wo2, g1, b1, wq, kt, v, wo, g2, b2, rw, rw1, rb, tm, rows_per_batch):
    t, d = x2d.shape
    blocks_per_batch = rows_per_batch // tm
    pieces = (d // 2) // SC_ROW_WORDS
    row = lambda w: pl.BlockSpec((tm, w), lambda i: (i, 0))
    full = lambda a: pl.BlockSpec(a.shape, lambda i: (0, 0))
    per_batch = lambda a: pl.BlockSpec((1,) + a.shape[1:], lambda i: (i // blocks_per_batch, 0, 0))
    return pl.pallas_call(
        _mid_kernel,
        grid=(t // tm,),
        in_specs=[row(d), row(og.shape[1]), row(od.shape[1]), full(wo1), full(wo2), full(g1), full(b1),
                  full(wq), per_batch(kt), per_batch(v), full(wo), full(g2), full(b2), full(rw), full(rw1), full(rb)],
        out_specs=[row(d), pl.BlockSpec((pieces, tm, SC_ROW_WORDS), lambda i: (0, i, 0)),
                   pl.BlockSpec((2 * TOP_K, tm), lambda i: (0, i)), row(LANES),
                   pl.BlockSpec((1, LANES), lambda i: (0, 0))],
        out_shape=[jax.ShapeDtypeStruct((t, d), F32), jax.ShapeDtypeStruct((pieces, t, SC_ROW_WORDS), U32),
                   jax.ShapeDtypeStruct((2 * TOP_K, t), I32), jax.ShapeDtypeStruct((t, LANES), F32),
                   jax.ShapeDtypeStruct((1, LANES), I32)],
        scratch_shapes=[pltpu.VMEM((1, LANES), F32)],
        compiler_params=pltpu.CompilerParams(dimension_semantics=("arbitrary",),
                                             vmem_limit_bytes=VMEM_LIMIT),
        name="mid",
    )(x2d, og, od, wo1, wo2, g1, b1, wq, kt, v, wo, g2, b2, rw, rw1, rb)


def _sc_scatter_rows(rows, idx4, n_out):
    t, d = rows.shape
    mesh = plsc.VectorSubcoreMesh(core_axis_name="c", subcore_axis_name="s")

    @functools.partial(pl.kernel, out_type=jax.ShapeDtypeStruct((n_out, d), rows.dtype), mesh=mesh)
    def scatter_kernel(x_hbm, i0_hbm, i1_hbm, i2_hbm, i3_hbm, o_hbm):
        def body(x_vmem, i0, i1, i2, i3):
            for iv in (i0, i1, i2, i3):
                pltpu.sync_copy(x_vmem, o_hbm.at[iv.at[0]])

        ispec = pl.BlockSpec((1, SC_WINDOW), lambda i: (0, i))
        pltpu.emit_pipeline(
            body,
            grid=(t // SC_WINDOW,),
            in_specs=[pl.BlockSpec((SC_WINDOW, d), lambda i: (i, 0)), ispec, ispec, ispec, ispec],
            out_specs=[],
            core_axis_name=("c", "s"),
            dimension_semantics=(pltpu.PARALLEL,),
        )(x_hbm, i0_hbm, i1_hbm, i2_hbm, i3_hbm)

    return scatter_kernel(rows, *[idx4[k:k + 1] for k in range(TOP_K)])


def _sc_gather_rows(table, idx):
    n = idx.shape[0]
    d = table.shape[1]
    mesh = plsc.VectorSubcoreMesh(core_axis_name="c", subcore_axis_name="s")

    @functools.partial(pl.kernel, out_type=jax.ShapeDtypeStruct((n, d), table.dtype), mesh=mesh)
    def gather_kernel(x_hbm, i_hbm, o_hbm):
        def body(i_vmem, o_vmem):
            pltpu.sync_copy(x_hbm.at[i_vmem.at[0]], o_vmem)

        pltpu.emit_pipeline(
            body,
            grid=(n // SC_WINDOW,),
            in_specs=[pl.BlockSpec((1, SC_WINDOW), lambda i: (0, i))],
            out_specs=[pl.BlockSpec((SC_WINDOW, d), lambda i: (i, 0))],
            core_axis_name=("c", "s"),
            dimension_semantics=(pltpu.PARALLEL,),
        )(i_hbm, o_hbm)

    return gather_kernel(table, idx.reshape(1, n))


def _moe_kernel(be_ref, nu_ref, nx_ref, x_ref, wg_hbm, bg_ref, wu_hbm, bu_ref, wd_hbm, bd_ref, y_ref,
                stage_ref, slot_ref, sem):
    i = pl.program_id(0)
    used = i < nu_ref[0]
    expert = be_ref[i]
    prev = be_ref[jnp.maximum(i - 1, 0)]
    changed = jnp.logical_and(used, jnp.logical_or(i == 0, expert != prev))

    def weight_copies(e, slot):
        return [pltpu.make_async_copy(w.at[e], stage_ref.at[slot, k], sem.at[slot, k])
                for k, w in enumerate((wg_hbm, wu_hbm, wd_hbm))]

    @pl.when(changed)
    def _():
        @pl.when(i == 0)
        def _():
            slot_ref[0] = 1
            for cp in weight_copies(expert, 0):
                cp.start()

        slot = 1 - slot_ref[0]
        slot_ref[0] = slot
        for cp in weight_copies(expert, slot):
            cp.wait()
        nxt = nx_ref[i]

        @pl.when(nxt >= 0)
        def _():
            for cp in weight_copies(nxt, 1 - slot):
                cp.start(priority=WEIGHT_DMA_PRIORITY)

    @pl.when(used)
    def _():
        slot = slot_ref[0]
        halves = [_unpack_bf16_pairs(x_ref[piece]) for piece in range(x_ref.shape[0])]
        x = jnp.concatenate([h[0] for h in halves] + [h[1] for h in halves], axis=-1).astype(BF16)
        hg = _dot(x, stage_ref[slot, 0].astype(BF16)) + bg_ref[0]
        hl = _dot(x, stage_ref[slot, 1].astype(BF16)) + bu_ref[0]
        hg = jnp.minimum(hg, SWIGLU_LIMIT)
        hl = jnp.clip(hl, -SWIGLU_LIMIT, SWIGLU_LIMIT)
        act = hg * _sigmoid(SWIGLU_ALPHA * hg) * (hl + 1.0)
        y = _dot(act.astype(BF16), stage_ref[slot, 2].astype(BF16)) + bd_ref[0]
        packed = _pack_bf16_pairs(y)
        for piece in range(y_ref.shape[0]):
            y_ref[piece] = packed[:, piece * SC_ROW_WORDS:(piece + 1) * SC_ROW_WORDS]

    @pl.when(jnp.logical_not(used))
    def _():
        y_ref[...] = jnp.zeros_like(y_ref)


def _moe(block_e, n_used, next_e, xs, wg, bg, wu, bu, wd, bd, tm):
    pieces, p, dp = xs.shape
    d, dff = wg.shape[1], wg.shape[2]
    assert d == dff, "the shared f32 staging buffer assumes square expert matrices"
    n_blocks = p // tm
    bspec = lambda a: pl.BlockSpec((1,) + a.shape[1:], lambda i, be, nu, nx: (be[i], 0, 0))
    hbm = pl.BlockSpec(memory_space=pl.ANY)
    rows = pl.BlockSpec((pieces, tm, dp), lambda i, be, nu, nx: (0, i, 0))
    grid_spec = pltpu.PrefetchScalarGridSpec(
        num_scalar_prefetch=3,
        grid=(n_blocks,),
        in_specs=[rows, hbm, bspec(bg), hbm, bspec(bu), hbm, bspec(bd)],
        out_specs=rows,
        scratch_shapes=[pltpu.VMEM((2, 3, d, dff), F32), pltpu.SMEM((1,), I32),
                        pltpu.SemaphoreType.DMA((2, 3))],
    )
    return pl.pallas_call(
        _moe_kernel,
        grid_spec=grid_spec,
        out_shape=jax.ShapeDtypeStruct((pieces, p, dp), U32),
        compiler_params=pltpu.CompilerParams(dimension_semantics=("arbitrary",),
                                             vmem_limit_bytes=VMEM_LIMIT),
        name="moe_ffn",
    )(block_e, n_used, next_e, xs, wg, bg, wu, bu, wd, bd)


def _final_kernel(x_ref, y4_ref, gate_ref, g_ref, b_ref, o_ref):
    pieces = y4_ref.shape[0]
    gates = gate_ref[...]
    lows = [0.0] * pieces
    highs = [0.0] * pieces
    for kk in range(TOP_K):
        gk = gates[:, kk:kk + 1]
        for piece in range(pieces):
            lo, hi = _unpack_bf16_pairs(y4_ref[piece, kk])
            lows[piece] = lows[piece] + gk * lo
            highs[piece] = highs[piece] + gk * hi
    y = jnp.concatenate(lows + highs, axis=-1)
    o_ref[...] = _layer_norm(DEEPNORM_ALPHA * x_ref[...] + y, g_ref[...], b_ref[...])


def _final(x2, y4, gates, g, b, tm):
    t, d = x2.shape
    row = lambda w: pl.BlockSpec((tm, w), lambda i: (i, 0))
    full = lambda a: pl.BlockSpec(a.shape, lambda i: (0, 0))
    return pl.pallas_call(
        _final_kernel,
        grid=(t // tm,),
        in_specs=[row(d), pl.BlockSpec(y4.shape[:2] + (tm, y4.shape[3]), lambda i: (0, 0, i, 0)),
                  row(gates.shape[1]), full(g), full(b)],
        out_specs=row(d),
        out_shape=jax.ShapeDtypeStruct((t, d), F32),
        compiler_params=pltpu.CompilerParams(dimension_semantics=("parallel",),
                                             vmem_limit_bytes=VMEM_LIMIT),
        name="final_ln",
    )(x2, y4, gates, g, b)


def _layout(route, counts, tm):
    t = route.shape[1]
    experts = route[:TOP_K]
    ranks = route[TOP_K:]
    counts = counts[0, :N_EXPERTS]
    padded = (counts + tm - 1) // tm * tm
    pad_end = jnp.cumsum(padded)
    pad_start = pad_end - padded
    eids = jnp.arange(N_EXPERTS, dtype=I32)[:, None, None]
    dest = ranks + jnp.sum(jnp.where(experts[None] == eids, pad_start[:, None, None], 0), axis=0)
    n_blocks = t * TOP_K // tm + N_EXPERTS
    starts = jnp.arange(n_blocks, dtype=I32) * tm
    block_e = jnp.minimum(jnp.sum((pad_end[None, :] <= starts[:, None]).astype(I32), axis=-1), N_EXPERTS - 1)
    n_used = (pad_end[-1] // tm).astype(I32).reshape(1)
    e_row = jnp.arange(N_EXPERTS, dtype=I32)
    later = (e_row[None, :] > e_row[:, None]) & (padded[None, :] > 0)
    next_of = jnp.min(jnp.where(later, e_row[None, :], N_EXPERTS), axis=1)
    next_of = jnp.where(next_of == N_EXPERTS, -1, next_of)
    next_e = jnp.sum(jnp.where(block_e[:, None] == e_row[None, :], next_of[None, :], 0), axis=1)
    return dest.astype(I32), block_e.astype(I32), n_used, next_e.astype(I32), n_blocks


def kernel(x, mem, w_in, gdn_conv_w, gdn_a_log, gdn_dt_bias, gdn_norm_w, diff_lq1, diff_lk1, diff_lq2, diff_lk2,
           diff_norm_w, w_out, ln1_g, ln1_b, xa_wq, xa_wk, xa_wv, xa_wo, ln2_g, ln2_b, router_w, router_b,
           exp_w_gate, exp_b_gate, exp_w_up, exp_b_up, exp_w_down, exp_b_down, ln3_g, ln3_b):
    b, s, d = x.shape
    t = b * s
    assert all(s % rows == 0 for rows in (INPROJ_ROWS, GDN_ROWS, DIFF_QUERY_ROWS, MID_ROWS)), s
    assert t % FINAL_ROWS == 0 and (t * TOP_K) % MOE_ROWS == 0 and t % SC_WINDOW == 0, t
    assert d % (2 * SC_ROW_WORDS) == 0 and d % XA_HEADS == 0, d
    x2d = x.reshape(t, d)
    for l in range(DEPTH):
        lambda_init = 0.8 - 0.6 * math.exp(-0.3 * l)
        n_g = 4 * GDN_W
        w = w_in[l]
        wg = w[:, :n_g].astype(BF16)
        small = w[:, n_g:n_g + 2 * GDN_HEADS]
        ws = jnp.pad(small, ((0, 0), (0, SMALL_W - 2 * GDN_HEADS))).astype(BF16)
        d0 = n_g + 2 * GDN_HEADS
        wdq = w[:, d0:d0 + DIFF_W].astype(BF16)
        wdk = w[:, d0 + DIFF_W:d0 + 2 * DIFF_W].astype(BF16)
        wdv = w[:, d0 + 2 * DIFF_W:d0 + 3 * DIFF_W].astype(BF16)
        conv_w = gdn_conv_w[l].reshape(GDN_CONV, 3 * GDN_W).astype(F32)
        ab = jnp.zeros((2, SMALL_W), F32)
        ab = ab.at[0, GDN_HEADS:2 * GDN_HEADS].set(-jnp.exp(gdn_a_log[l].astype(F32)))
        ab = ab.at[1, GDN_HEADS:2 * GDN_HEADS].set(gdn_dt_bias[l].astype(F32))
        lam = (jnp.exp(jnp.sum(diff_lq1[l].astype(F32) * diff_lk1[l].astype(F32)))
               - jnp.exp(jnp.sum(diff_lq2[l].astype(F32) * diff_lk2[l].astype(F32))) + lambda_init).reshape(1)

        g2d, dqt, dk2d, dvt, s2d = _inproj(x2d, wg, wdq, wdk, wdv, ws, conv_w, tm=INPROJ_ROWS, batch=b,
                                           q_scale=LOG2E * DIFF_DQK ** -0.5)
        o_gdn = _gdn(g2d.reshape(b, s, n_g), s2d.reshape(b, s, SMALL_W), ab,
                     gdn_norm_w[l].reshape(1, GDN_DV).astype(F32), ts=GDN_ROWS)
        o_diff = _diff_attn(lam, dqt, dk2d.reshape(b, s, DIFF_W), dvt,
                            diff_norm_w[l].reshape(DIFF_DV, 1).astype(F32), tq=DIFF_QUERY_ROWS,
                            out_scale=1.0 - lambda_init)

        dh = d // XA_HEADS
        kt, v = _memkv(mem, xa_wk[l].astype(BF16), xa_wv[l].astype(BF16), scale=dh ** -0.5)
        wo_mix = w_out[l].astype(BF16)
        rw_f = router_w[l].astype(F32)
        rw_hi = rw_f.astype(BF16)
        rw_lo = (rw_f - rw_hi.astype(F32)).astype(BF16)
        rw = jnp.pad(jnp.concatenate([rw_hi, rw_lo], axis=1), ((0, 0), (0, LANES - 2 * N_EXPERTS)))
        rw1 = jnp.pad(rw_hi, ((0, 0), (0, LANES - N_EXPERTS)))
        rb = jnp.pad(router_b[l].astype(F32), (0, LANES - N_EXPERTS)).reshape(1, LANES)
        row = lambda a: a.reshape(1, -1).astype(F32)
        x2, x2p, route, gates, counts = _mid(
            x2d, o_gdn.reshape(t, GDN_W), o_diff.reshape(t, DIFF_HEADS * DIFF_DV),
            wo_mix[:GDN_W], wo_mix[GDN_W:], row(ln1_g[l]), row(ln1_b[l]),
            xa_wq[l].astype(BF16), kt, v, xa_wo[l].astype(BF16), row(ln2_g[l]), row(ln2_b[l]),
            rw, rw1, rb, tm=MID_ROWS, rows_per_batch=s)

        tm_moe = MOE_ROWS
        dest, block_e, n_used, next_e, n_blocks = _layout(route, counts, tm_moe)
        pieces = x2p.shape[0]
        p_rows = n_blocks * tm_moe
        offs = (jnp.arange(pieces, dtype=I32) * p_rows)[:, None]
        idx_scatter = (dest[:, None, :] + offs[None]).reshape(TOP_K, pieces * t)
        xs = _sc_scatter_rows(x2p.reshape(pieces * t, SC_ROW_WORDS), idx_scatter,
                              pieces * p_rows).reshape(pieces, p_rows, SC_ROW_WORDS)
        ys = _moe(block_e, n_used, next_e, xs,
                  exp_w_gate[l], exp_b_gate[l].reshape(N_EXPERTS, 1, -1), exp_w_up[l],
                  exp_b_up[l].reshape(N_EXPERTS, 1, -1), exp_w_down[l], exp_b_down[l].reshape(N_EXPERTS, 1, -1),
                  tm=tm_moe)
        idx_gather = (dest.reshape(1, TOP_K * t) + offs).reshape(pieces * TOP_K * t)
        y4 = _sc_gather_rows(ys.reshape(pieces * p_rows, SC_ROW_WORDS),
                             idx_gather).reshape(pieces, TOP_K, t, SC_ROW_WORDS)
        x2d = _final(x2, y4, gates, row(ln3_g[l]), row(ln3_b[l]), tm=FINAL_ROWS)
    return x2d.reshape(b, s, d)
```
